```python
import math
import jax, jax.numpy as jnp
from jax import lax
import numpy as np

D_MODEL = 2048
BATCH = 4
SEQ = 4096
DEPTH = 1
DEC_BATCH = 128
DEC_SEQ = 8
PAST_LEN = 16384
PAGE_SIZE = 128

ATT_HEADS = 32
ATT_KV_HEADS = 8
ATT_GROUP = ATT_HEADS // ATT_KV_HEADS
ATT_HEAD_DIM = 64
WINDOW = 128
ATT_BLOCK = WINDOW
N_BUCKETS = 32
MAX_DISTANCE = 128
ML_HEADS = 4
ML_DK = 256
ML_DV = 512
ML_CONV = 4
ML_CHUNK = 64
PEER_HEADS = 8
PEER_NKEYS = 128
PEER_N_EXPERTS = PEER_NKEYS * PEER_NKEYS
PEER_TOPK = 16
PEER_DQ = 256
PEER_BLOCK = 64

ATT_Q_W = ATT_HEADS * ATT_HEAD_DIM
ATT_KV_W = ATT_KV_HEADS * ATT_HEAD_DIM
ML_QK_W = ML_HEADS * ML_DK
ML_V_W = ML_HEADS * ML_DV
IN_W = ATT_Q_W + 2 * ATT_KV_W + 2 * ML_QK_W + 2 * ML_V_W + 2 * ML_HEADS + 2 * D_MODEL
ALPHA = (2.0 * DEPTH) ** 0.25
BETA = (8.0 * DEPTH) ** -0.25
LN_EPS = 1e-5
NEG = -1e30

kernel_name = 'hybrid_swa_mlstm_peer_step'


def _split_points():
    widths = (ATT_Q_W, ATT_KV_W, ATT_KV_W, 2 * ML_QK_W, ML_V_W, ML_HEADS, ML_HEADS, ML_V_W, D_MODEL, D_MODEL)
    pts, acc = [], 0
    for w in widths[:-1]:
        acc += w
        pts.append(acc)
    return pts


def layer_norm(x, g, b):
    xf = x.astype(jnp.float32)
    mu = xf.mean(-1, keepdims=True)
    var = jnp.mean(jnp.square(xf - mu), -1, keepdims=True)
    return ((xf - mu) * lax.rsqrt(var + LN_EPS) * g + b).astype(x.dtype)


def head_norm(h, g):
    mu = h.mean(-1, keepdims=True)
    var = jnp.mean(jnp.square(h - mu), -1, keepdims=True)
    return (h - mu) * lax.rsqrt(var + LN_EPS) * g.reshape(ML_HEADS, ML_DV).astype(jnp.float32)


def t5_bucket(dist):
    max_exact = N_BUCKETS // 2
    d = jnp.maximum(dist, 0)
    dl = jnp.maximum(d, max_exact).astype(jnp.float32)
    large = max_exact + (jnp.log(dl / max_exact) / math.log(MAX_DISTANCE / max_exact)
                         * (N_BUCKETS - max_exact)).astype(jnp.int32)
    large = jnp.minimum(large, N_BUCKETS - 1)
    return jnp.where(d < max_exact, d, large)


def sink_attention(q, k, v, dist, mask, rel_table, sinks):
    lq, lk = dist.shape
    bias = jnp.transpose(rel_table[t5_bucket(dist)], (2, 0, 1))
    bias = bias.reshape(ATT_KV_HEADS, ATT_GROUP, lq, lk).astype(jnp.float32)
    s = jnp.einsum('bnqhgd,bnkhd->bnhgqk', q, k).astype(jnp.float32) * (ATT_HEAD_DIM ** -0.5) + bias
    s = jnp.where(mask, s, NEG)
    sink = sinks.astype(jnp.float32).reshape(ATT_KV_HEADS, ATT_GROUP, 1, 1)
    m = jnp.maximum(s.max(-1, keepdims=True), sink)
    p = jnp.exp(s - m)
    p = p / (p.sum(-1, keepdims=True) + jnp.exp(sink - m))
    return jnp.einsum('bnhgqk,bnkhd->bnqhgd', p.astype(v.dtype), v)


def swa_prompt(q, k, v, rel_table, sinks):
    B, T = q.shape[:2]
    nb = T // ATT_BLOCK
    qb = q.reshape(B, nb, ATT_BLOCK, ATT_KV_HEADS, ATT_GROUP, ATT_HEAD_DIM)
    kb = k.reshape(B, nb, ATT_BLOCK, ATT_KV_HEADS, ATT_HEAD_DIM)
    vb = v.reshape(B, nb, ATT_BLOCK, ATT_KV_HEADS, ATT_HEAD_DIM)
    pad = ((0, 0), (1, 0), (0, 0), (0, 0), (0, 0))
    kk = jnp.concatenate([jnp.pad(kb, pad)[:, :-1], kb], axis=2)
    vv = jnp.concatenate([jnp.pad(vb, pad)[:, :-1], vb], axis=2)
    qi = jnp.arange(ATT_BLOCK)[:, None] + ATT_BLOCK
    kj = jnp.arange(2 * ATT_BLOCK)[None, :]
    dist = qi - kj
    band = (dist >= 0) & (dist < WINDOW)
    has_prev = jnp.arange(nb)[:, None, None] > 0
    mask = band[None] & (has_prev | (kj >= ATT_BLOCK)[None])
    out = sink_attention(qb, kk, vv, dist, mask[None, :, None, None], rel_table, sinks)
    return out.reshape(B, T, ATT_KV_HEADS, ATT_GROUP, ATT_HEAD_DIM)


def swa_sample(q, k, v, kwin, vwin, rel_table, sinks):
    T = q.shape[1]
    wc = kwin.shape[1]
    kk = jnp.concatenate([kwin.astype(k.dtype), k], axis=1)
    vv = jnp.concatenate([vwin.astype(v.dtype), v], axis=1)
    dist = (jnp.arange(T)[:, None] + wc) - jnp.arange(wc + T)[None, :]
    mask = (dist >= 0) & (dist < WINDOW)
    out = sink_attention(q[:, None], kk[:, None], vv[:, None], dist, mask, rel_table, sinks)[:, 0]
    return out, kk[:, T:], vv[:, T:]


def causal_conv(x, buf, w, b):
    T = x.shape[1]
    xx = jnp.concatenate([buf.astype(x.dtype), x], axis=1)
    y = b + xx[:, 0:T] * w[0]
    for j in range(1, ML_CONV):
        y = y + xx[:, j:j + T] * w[j]
    return y, xx[:, T:]


def _to_chunks(a, nc, L):
    a = a.reshape((a.shape[0], nc, L) + a.shape[2:])
    return jnp.moveaxis(jnp.moveaxis(a, 1, 0), 2, 3)


def mlstm_chunkwise(q, k, v, i_pre, logf, C0, n0, m0):
    B, T = q.shape[:2]
    L = min(ML_CHUNK, T)
    nc = -(-T // L)
    pad = nc * L - T
    p4 = ((0, 0), (0, pad), (0, 0), (0, 0))
    p3 = ((0, 0), (0, pad), (0, 0))
    qc = _to_chunks(jnp.pad(q, p4), nc, L)
    kc = _to_chunks(jnp.pad(k, p4), nc, L)
    vc = _to_chunks(jnp.pad(v, p4), nc, L)
    ic = _to_chunks(jnp.pad(i_pre, p3, constant_values=NEG), nc, L)
    fc = _to_chunks(jnp.pad(logf, p3), nc, L)
    tri = jnp.tril(jnp.ones((L, L), bool))

    def step(carry, xs):
        C, n, m = carry
        qx, kx, vx, ix, fx = xs
        bcum = jnp.cumsum(fx, axis=-1)
        dmat = jnp.where(tri, bcum[..., :, None] - bcum[..., None, :] + ix[..., None, :], NEG)
        inter = bcum + m[..., None]
        mt = jnp.maximum(inter, dmat.max(-1))
        s = jnp.einsum('bhtd,bhsd->bhts', qx, kx) * jnp.exp(dmat - mt[..., None])
        w_in = jnp.exp(inter - mt)
        num = jnp.einsum('bhts,bhsv->bhtv', s, vx) + w_in[..., None] * jnp.einsum('bhtd,bhdv->bhtv', qx, C)
        den = s.sum(-1) + w_in * jnp.einsum('bhtd,bhd->bht', qx, n)
        h = num / jnp.maximum(jnp.abs(den), jnp.exp(-mt))[..., None]
        b_last = bcum[..., -1]
        dec = b_last[..., None] - bcum + ix
        m_new = jnp.maximum(b_last + m, dec.max(-1))
        wk = jnp.exp(dec - m_new[..., None])
        decay = jnp.exp(b_last + m - m_new)
        C_new = decay[..., None, None] * C + jnp.einsum('bhsd,bhsv->bhdv', wk[..., None] * kx, vx)
        n_new = decay[..., None] * n + jnp.einsum('bhs,bhsd->bhd', wk, kx)
        return (C_new, n_new, m_new), h

    (C1, n1, m1), hs = lax.scan(step, (C0, n0, m0), (qc, kc, vc, ic, fc))
    h = jnp.transpose(hs, (1, 0, 3, 2, 4)).reshape(B, nc * L, ML_HEADS, ML_DV)[:, :T]
    return h, C1, n1, m1


def token_mixer(h, state, rel_table, w_in, b_i, b_f, conv_w, conv_b, ml_norm_g, sinks, w_a, w_b, w_out):
    B, T, _ = h.shape
    aq, ak, av, mqk, mv, mi, mf, mo, ga, gb = jnp.split(h @ w_in, _split_points(), axis=-1)
    aq = aq.reshape(B, T, ATT_KV_HEADS, ATT_GROUP, ATT_HEAD_DIM)
    ak = ak.reshape(B, T, ATT_KV_HEADS, ATT_HEAD_DIM)
    av = av.reshape(B, T, ATT_KV_HEADS, ATT_HEAD_DIM)
    if state is None:
        att = swa_prompt(aq, ak, av, rel_table, sinks)
        keep = min(WINDOW, T)
        k_keep, v_keep = ak[:, T - keep:], av[:, T - keep:]
        conv_buf = jnp.zeros((B, ML_CONV - 1, 2 * ML_QK_W), h.dtype)
        C0 = jnp.zeros((B, ML_HEADS, ML_DK, ML_DV), jnp.float32)
        n0 = jnp.zeros((B, ML_HEADS, ML_DK), jnp.float32)
        m0 = jnp.zeros((B, ML_HEADS), jnp.float32)
    else:
        kwin, vwin, conv_buf, C0, n0, m0 = state
        att, k_keep, v_keep = swa_sample(aq, ak, av, kwin, vwin, rel_table, sinks)
    y_a = att.reshape(B, T, ATT_Q_W) @ w_a
    qk, conv_keep = causal_conv(mqk, conv_buf, conv_w, conv_b)
    qk = jax.nn.silu(qk).astype(jnp.float32)
    mq = qk[..., :ML_QK_W].reshape(B, T, ML_HEADS, ML_DK)
    mk = qk[..., ML_QK_W:].reshape(B, T, ML_HEADS, ML_DK) * (ML_DK ** -0.5)
    mvv = mv.astype(jnp.float32).reshape(B, T, ML_HEADS, ML_DV)
    i_pre = (mi + b_i).astype(jnp.float32)
    logf = jax.nn.log_sigmoid((mf + b_f).astype(jnp.float32))
    hm, C1, n1, m1 = mlstm_chunkwise(mq, mk, mvv, i_pre, logf, C0.astype(jnp.float32),
                                     n0.astype(jnp.float32), m0.astype(jnp.float32))
    hm = head_norm(hm, ml_norm_g).reshape(B, T, ML_V_W).astype(h.dtype)
    y_b = (jax.nn.sigmoid(mo) * hm) @ w_b
    y = (jax.nn.sigmoid(ga) * y_a + jax.nn.sigmoid(gb) * y_b) @ w_out
    return y, (k_keep, v_keep, conv_keep, C1, n1, m1)


def peer_ffn(h, w_pq, sub_keys, peer_u, peer_v):
    B, T, D = h.shape
    xt = h.reshape(B * T, D)
    n_tok = B * T
    npad = (-n_tok) % PEER_BLOCK
    xt = jnp.pad(xt, ((0, npad), (0, 0))).reshape(-1, PEER_BLOCK, D)

    def block(xb):
        q = (xb @ w_pq).reshape(PEER_BLOCK, PEER_HEADS, 2, PEER_DQ // 2)
        s = jnp.einsum('thpd,hpkd->thpk', q, sub_keys).astype(jnp.float32)
        sv, si = lax.top_k(s, PEER_TOPK)
        cand = (sv[..., 0, :, None] + sv[..., 1, None, :]).reshape(PEER_BLOCK, PEER_HEADS, PEER_TOPK * PEER_TOPK)
        cid = (si[..., 0, :, None] * PEER_NKEYS + si[..., 1, None, :]).reshape(PEER_BLOCK, PEER_HEADS, PEER_TOPK * PEER_TOPK)
        top_s, pos = lax.top_k(cand, PEER_TOPK)
        eid = jnp.take_along_axis(cid, pos, axis=-1)
        g = jax.nn.softmax(top_s, axis=-1)
        act = jax.nn.gelu(jnp.einsum('thkd,td->thk', peer_u[eid], xb).astype(jnp.float32), approximate=False)
        return jnp.einsum('thk,thkd->td', (g * act).astype(xb.dtype), peer_v[eid])

    out = lax.map(block, xt)
    return out.reshape(-1, D)[:n_tok].reshape(B, T, D)


def decoder_layer(x, c, state, prm):
    (rel_table, w_ada, b_ada, w_in, b_i, b_f, conv_w, conv_b, ml_norm_g, sinks,
     w_a, w_b, w_out, ln1_g, ln1_b, w_pq, sub_keys, peer_u, peer_v, ln2_g, ln2_b) = prm
    ada = jax.nn.silu(c) @ w_ada + b_ada
    sh1, sc1, g1, sh2, sc2, g2 = jnp.split(ada[:, None, :], 6, axis=-1)
    h1 = x * (1 + sc1) + sh1
    mix, new_state = token_mixer(h1, state, rel_table, w_in, b_i, b_f, conv_w, conv_b, ml_norm_g,
                                 sinks, w_a, w_b, w_out)
    x = layer_norm(ALPHA * x + g1 * mix, ln1_g, ln1_b)
    h2 = x * (1 + sc2) + sh2
    x = layer_norm(ALPHA * x + g2 * peer_ffn(h2, w_pq, sub_keys, peer_u, peer_v), ln2_g, ln2_b)
    return x, new_state


def _stack_states(per_layer):
    return [jnp.stack([s[i] for s in per_layer]) for i in range(6)]


def setup_inputs(seed: int = 0) -> dict:
    key = jax.random.key(seed)
    ks = jax.random.split(key, 40)
    f32 = jnp.float32

    def nrm(k, shape, scale):
        return jax.random.normal(k, shape, f32) * scale

    W_C = min(WINDOW, PAST_LEN)
    D = D_MODEL
    return {
        'x_prompt': nrm(ks[0], (BATCH, SEQ, D), 1.0),
        'x_sample': nrm(ks[1], (DEC_BATCH, DEC_SEQ, D), 1.0),
        'c_prompt': nrm(ks[2], (BATCH, D), 1.0),
        'c_sample': nrm(ks[3], (DEC_BATCH, D), 1.0),
        'cache_k_win': nrm(ks[4], (DEPTH, DEC_BATCH, W_C, ATT_KV_HEADS, ATT_HEAD_DIM), 1.0),
        'cache_v_win': nrm(ks[5], (DEPTH, DEC_BATCH, W_C, ATT_KV_HEADS, ATT_HEAD_DIM), 1.0),
        'state_conv': nrm(ks[6], (DEPTH, DEC_BATCH, ML_CONV - 1, 2 * ML_QK_W), 1.0),
        'state_C': nrm(ks[7], (DEPTH, DEC_BATCH, ML_HEADS, ML_DK, ML_DV), 0.1),
        'state_n': nrm(ks[8], (DEPTH, DEC_BATCH, ML_HEADS, ML_DK), 0.1),
        'state_m': nrm(ks[9], (DEPTH, DEC_BATCH, ML_HEADS), 0.5),
        'rel_bias_table': nrm(ks[10], (N_BUCKETS, ATT_HEADS), 0.5),
        'w_ada': nrm(ks[11], (DEPTH, D, 6 * D), 0.5 * D ** -0.5),
        'b_ada': nrm(ks[12], (DEPTH, 6 * D), 0.02),
        'w_in': nrm(ks[13], (DEPTH, D, IN_W), D ** -0.5),
        'b_i': nrm(ks[14], (DEPTH, ML_HEADS), 0.1),
        'b_f': 3.0 + nrm(ks[15], (DEPTH, ML_HEADS), 0.1),
        'conv_w': nrm(ks[16], (DEPTH, ML_CONV, 2 * ML_QK_W), ML_CONV ** -0.5),
        'conv_b': nrm(ks[17], (DEPTH, 2 * ML_QK_W), 0.02),
        'ml_norm_g': 1.0 + nrm(ks[18], (DEPTH, ML_V_W), 0.02),
        'att_sinks': nrm(ks[19], (DEPTH, ATT_HEADS), 0.5),
        'w_a': nrm(ks[20], (DEPTH, ATT_Q_W, D), BETA * ATT_Q_W ** -0.5),
        'w_b': nrm(ks[21], (DEPTH, ML_V_W, D), BETA * ML_V_W ** -0.5),
        'w_out': nrm(ks[22], (DEPTH, D, D), BETA * D ** -0.5),
        'ln1_g': 1.0 + nrm(ks[23], (DEPTH, D), 0.02),
        'ln1_b': nrm(ks[24], (DEPTH, D), 0.02),
        'w_pq': nrm(ks[25], (DEPTH, D, PEER_HEADS * PEER_DQ), D ** -0.5),
        'peer_sub_keys': nrm(ks[26], (DEPTH, PEER_HEADS, 2, PEER_NKEYS, PEER_DQ // 2), (PEER_DQ // 2) ** -0.5),
        'peer_u': nrm(ks[27], (DEPTH, PEER_N_EXPERTS, D), D ** -0.5),
        'peer_v': nrm(ks[28], (DEPTH, PEER_N_EXPERTS, D), BETA * PEER_HEADS ** -0.5),
        'ln2_g': 1.0 + nrm(ks[29], (DEPTH, D), 0.02),
        'ln2_b': nrm(ks[30], (DEPTH, D), 0.02),
    }


def reference(x_prompt, x_sample, c_prompt, c_sample, cache_k_win, cache_v_win, state_conv,
              state_C, state_n, state_m, rel_bias_table, w_ada, b_ada, w_in, b_i, b_f, conv_w, conv_b,
              ml_norm_g, att_sinks, w_a, w_b, w_out, ln1_g, ln1_b, w_pq, peer_sub_keys, peer_u, peer_v,
              ln2_g, ln2_b):
    yp, ys = x_prompt, x_sample
    new_p, new_s = [], []
    for l in range(DEPTH):
        prm = (rel_bias_table, w_ada[l], b_ada[l], w_in[l], b_i[l], b_f[l], conv_w[l], conv_b[l],
               ml_norm_g[l], att_sinks[l], w_a[l], w_b[l], w_out[l], ln1_g[l], ln1_b[l], w_pq[l],
               peer_sub_keys[l], peer_u[l], peer_v[l], ln2_g[l], ln2_b[l])
        yp, sp = decoder_layer(yp, c_prompt, None, prm)
        ys, ss = decoder_layer(ys, c_sample, (cache_k_win[l], cache_v_win[l], state_conv[l],
                                              state_C[l], state_n[l], state_m[l]), prm)
        new_p.append(sp)
        new_s.append(ss)
    kw_p, vw_p, conv_p, C_p, n_p, m_p = _stack_states(new_p)
    kw_s, vw_s, conv_s, C_s, n_s, m_s = _stack_states(new_s)
    return (yp, ys, kw_p, vw_p, conv_p, C_p, n_p, m_p, kw_s, vw_s, conv_s, C_s, n_s, m_s)
```

```python
import functools
import math

import jax
import jax.numpy as jnp
from jax import lax
from jax.experimental import pallas as pl
from jax.experimental.pallas import tpu as pltpu

F32 = jnp.float32
BF16 = jnp.bfloat16

ATT_HEADS = 32
ATT_KV = 8
ATT_G = ATT_HEADS // ATT_KV
HD = 64
WINDOW = 128
N_BUCKETS = 32
MAX_DISTANCE = 128
ML_H = 4
ML_DK = 256
ML_DV = 512
ML_CONV = 4
PEER_H = 8
NK = 128
TOPK = 16
ATT_Q_W = ATT_HEADS * HD
ATT_KV_W = ATT_KV * HD
ML_QK_W = ML_H * ML_DK
ML_V_W = ML_H * ML_DV
DEPTH = 1
ALPHA = (2.0 * DEPTH) ** 0.25
LN_EPS = 1e-5
NEG = -1e30

COL_AQ = 0
COL_MQK = 2048
COL_MV = 4096
COL_MO = 6144
COL_GA = 8192
COL_GB = 10240
COL_AK = 12288
COL_AV = 12800
COL_IF = 13312
IN_PAD = 13440
IN_TN = 1920

ML_CHUNK_MAX = 256
VMEM_LIMIT = 56 * 1024 * 1024


def _cparams(sem):
    return pltpu.CompilerParams(dimension_semantics=sem, vmem_limit_bytes=VMEM_LIMIT)


def _sigmoid(x):
    return 1.0 / (1.0 + jnp.exp(-x))


def _silu(x):
    return x * _sigmoid(x)


def _log_sigmoid(x):
    return jnp.minimum(x, 0.0) - jnp.log1p(jnp.exp(-jnp.abs(x)))


def _ada_kernel(c_ref, w_ref, b_ref, o_ref):
    c = c_ref[...]
    a = _silu(c).astype(BF16)
    o_ref[...] = jnp.dot(a, w_ref[...].astype(BF16), preferred_element_type=F32) + b_ref[...]


def _ada(c_all, w_ada, b_ada):
    r, d = c_all.shape
    n = w_ada.shape[1]
    tn = 1024
    return pl.pallas_call(
        _ada_kernel,
        grid=(n // tn,),
        in_specs=[pl.BlockSpec((r, d), lambda j: (0, 0)),
                  pl.BlockSpec((d, tn), lambda j: (0, j)),
                  pl.BlockSpec((1, tn), lambda j: (0, j))],
        out_specs=pl.BlockSpec((r, tn), lambda j: (0, j)),
        out_shape=jax.ShapeDtypeStruct((r, n), F32),
        compiler_params=_cparams(("parallel",)),
        name="ada",
    )(c_all, w_ada, b_ada.reshape(1, n))


def _inproj_kernel(x_ref, sc_ref, sh_ref, w_ref, p_ref, h_scr):
    @pl.when(pl.program_id(1) == 0)
    def _():
        h = x_ref[...] * (1.0 + sc_ref[...]) + sh_ref[...]
        h_scr[...] = h.reshape(h_scr.shape).astype(BF16)

    p_ref[...] = jnp.dot(h_scr[...], w_ref[...], preferred_element_type=F32)


def _inproj(x, sc, sh, w_all, bb, tt):
    bx, tx, d = x.shape
    tm = bb * tt
    ntok = bx * tx
    grid = (ntok // tm, IN_PAD // IN_TN)
    tpb = tx // tt
    xmap = lambda i, j: (i // tpb if bb == 1 else i, i % tpb if bb == 1 else 0, 0)
    mmap = lambda i, j: (i // tpb if bb == 1 else i, 0, 0)
    return pl.pallas_call(
        _inproj_kernel,
        grid=grid,
        in_specs=[pl.BlockSpec((bb, tt, d), xmap),
                  pl.BlockSpec((bb, 1, d), mmap),
                  pl.BlockSpec((bb, 1, d), mmap),
                  pl.BlockSpec((d, IN_TN), lambda i, j: (0, j))],
        out_specs=pl.BlockSpec((tm, IN_TN), lambda i, j: (i, j)),
        out_shape=jax.ShapeDtypeStruct((ntok, IN_PAD), F32),
        scratch_shapes=[pltpu.VMEM((tm, d), BF16)],
        compiler_params=_cparams(("parallel", "arbitrary")),
        name="inproj",
    )(x, sc, sh, w_all)


def _t5_bucket(dist):
    max_exact = N_BUCKETS // 2
    d = jnp.maximum(dist, 0)
    dl = jnp.maximum(d, max_exact).astype(F32)
    large = max_exact + (jnp.log(dl / max_exact) / math.log(MAX_DISTANCE / max_exact)
                         * (N_BUCKETS - max_exact)).astype(jnp.int32)
    large = jnp.minimum(large, N_BUCKETS - 1)
    return jnp.where(d < max_exact, d, large)


def _bias_table(rel_table, dist):
    lq, lk = dist.shape
    band = (dist >= 0) & (dist < WINDOW)
    bias = jnp.transpose(rel_table[_t5_bucket(dist)], (2, 0, 1)).astype(F32)
    bias = jnp.where(band[None], bias, NEG)
    return bias.reshape(ATT_KV, ATT_G * lq, lk)


def _softmax_sink_parts(s, sink):
    m = jnp.maximum(jnp.max(s, axis=-1, keepdims=True), sink)
    return m


def _attn_prompt_kernel(q_ref, kp_ref, kc_ref, vp_ref, vc_ref, bias_ref, sink_ref, o_ref):
    n = pl.program_id(1)
    blk = q_ref.shape[1]
    q = q_ref[0]
    kcat = jnp.concatenate([kp_ref[0], kc_ref[0]], axis=0).astype(BF16)
    vcat = jnp.concatenate([vp_ref[0], vc_ref[0]], axis=0).astype(BF16)
    col = lax.broadcasted_iota(jnp.int32, (1, 2 * blk), 1)
    pen = jnp.where((col < blk) & (n == 0), NEG, 0.0)
    outs = []
    for kv in range(ATT_KV):
        qs = jnp.concatenate(
            [q[:, (kv * ATT_G + g) * HD:(kv * ATT_G + g + 1) * HD] for g in range(ATT_G)], axis=0
        ).astype(BF16)
        kh = kcat[:, kv * HD:(kv + 1) * HD]
        vh = vcat[:, kv * HD:(kv + 1) * HD]
        s = lax.dot_general(qs, kh, (((1,), (1,)), ((), ())), preferred_element_type=F32)
        s = s * (HD ** -0.5) + bias_ref[kv] + pen
        sink = sink_ref[kv]
        m = jnp.maximum(jnp.max(s, axis=-1, keepdims=True), sink)
        p = jnp.exp(s - m)
        den = jnp.sum(p, axis=-1, keepdims=True) + jnp.exp(sink - m)
        o = jnp.dot(p.astype(BF16), vh, preferred_element_type=F32) / den
        for g in range(ATT_G):
            outs.append(o[g * blk:(g + 1) * blk])
    o_ref[0] = jnp.concatenate(outs, axis=-1).astype(BF16)


def _attn_prompt(p3, bias, sinkcol):
    b, t, _ = p3.shape
    blk = WINDOW
    nb = t // blk
    kcol, vcol = COL_AK // ATT_KV_W, COL_AV // ATT_KV_W
    return pl.pallas_call(
        _attn_prompt_kernel,
        grid=(b, nb),
        in_specs=[pl.BlockSpec((1, blk, ATT_Q_W), lambda i, n: (i, n, 0)),
                  pl.BlockSpec((1, blk, ATT_KV_W), lambda i, n: (i, jnp.maximum(n - 1, 0), kcol)),
                  pl.BlockSpec((1, blk, ATT_KV_W), lambda i, n: (i, n, kcol)),
                  pl.BlockSpec((1, blk, ATT_KV_W), lambda i, n: (i, jnp.maximum(n - 1, 0), vcol)),
                  pl.BlockSpec((1, blk, ATT_KV_W), lambda i, n: (i, n, vcol)),
                  pl.BlockSpec(bias.shape, lambda i, n: (0, 0, 0)),
                  pl.BlockSpec(sinkcol.shape, lambda i, n: (0, 0, 0))],
        out_specs=pl.BlockSpec((1, blk, ATT_Q_W), lambda i, n: (i, n, 0)),
        out_shape=jax.ShapeDtypeStruct((b, t, ATT_Q_W), BF16),
        compiler_params=_cparams(("parallel", "arbitrary")),
        name="attn_prompt",
    )(p3, p3, p3, p3, p3, bias, sinkcol)


def _attn_sample_kernel(q_ref, kn_ref, vn_ref, kw_ref, vw_ref, bw_ref, bn_ref, sink_ref,
                        o_ref, ko_ref, vo_ref):
    t = q_ref.shape[1]
    wc = kw_ref.shape[1]
    q = q_ref[0]
    kn = kn_ref[0]
    vn = vn_ref[0]
    kw = kw_ref[0]
    vw = vw_ref[0]
    ko_ref[0, 0:wc - t, :] = kw[t:, :]
    ko_ref[0, wc - t:wc, :] = kn
    vo_ref[0, 0:wc - t, :] = vw[t:, :]
    vo_ref[0, wc - t:wc, :] = vn
    knb, vnb, kwb, vwb = kn.astype(BF16), vn.astype(BF16), kw.astype(BF16), vw.astype(BF16)
    outs = []
    nt = (((1,), (1,)), ((), ()))
    for kv in range(ATT_KV):
        qs = jnp.concatenate(
            [q[:, (kv * ATT_G + g) * HD:(kv * ATT_G + g + 1) * HD] for g in range(ATT_G)], axis=0
        ).astype(BF16)
        sl = slice(kv * HD, (kv + 1) * HD)
        sw = lax.dot_general(qs, kwb[:, sl], nt, preferred_element_type=F32) * (HD ** -0.5) + bw_ref[kv]
        sn = lax.dot_general(qs, knb[:, sl], nt, preferred_element_type=F32) * (HD ** -0.5) + bn_ref[kv]
        sink = sink_ref[kv]
        m = jnp.maximum(jnp.maximum(jnp.max(sw, axis=-1, keepdims=True),
                                    jnp.max(sn, axis=-1, keepdims=True)), sink)
        pw = jnp.exp(sw - m)
        pn = jnp.exp(sn - m)
        den = (jnp.sum(pw, axis=-1, keepdims=True) + jnp.sum(pn, axis=-1, keepdims=True)
               + jnp.exp(sink - m))
        o = (jnp.dot(pw.astype(BF16), vwb[:, sl], preferred_element_type=F32)
             + jnp.dot(pn.astype(BF16), vnb[:, sl], preferred_element_type=F32)) / den
        for g in range(ATT_G):
            outs.append(o[g * t:(g + 1) * t])
    o_ref[0] = jnp.concatenate(outs, axis=-1).astype(o_ref.dtype)


def _attn_sample(p3, kwin, vwin, bias_w, bias_n, sinkcol):
    b, t, _ = p3.shape
    wc = kwin.shape[1]
    kcol, vcol = COL_AK // ATT_KV_W, COL_AV // ATT_KV_W
    seq3 = lambda i: (i, 0, 0)
    const3 = lambda i: (0, 0, 0)
    return pl.pallas_call(
        _attn_sample_kernel,
        grid=(b,),
        in_specs=[pl.BlockSpec((1, t, ATT_Q_W), seq3),
                  pl.BlockSpec((1, t, ATT_KV_W), lambda i: (i, 0, kcol)),
                  pl.BlockSpec((1, t, ATT_KV_W), lambda i: (i, 0, vcol)),
                  pl.BlockSpec((1, wc, ATT_KV_W), seq3),
                  pl.BlockSpec((1, wc, ATT_KV_W), seq3),
                  pl.BlockSpec(bias_w.shape, const3),
                  pl.BlockSpec(bias_n.shape, const3),
                  pl.BlockSpec(sinkcol.shape, const3)],
        out_specs=[pl.BlockSpec((1, t, ATT_Q_W), seq3),
                   pl.BlockSpec((1, wc, ATT_KV_W), seq3),
                   pl.BlockSpec((1, wc, ATT_KV_W), seq3)],
        out_shape=[jax.ShapeDtypeStruct((b, t, ATT_Q_W), F32),
                   jax.ShapeDtypeStruct((b, wc, ATT_KV_W), F32),
                   jax.ShapeDtypeStruct((b, wc, ATT_KV_W), F32)],
        compiler_params=_cparams(("parallel",)),
        name="attn_sample",
    )(p3, p3, p3, kwin, vwin, bias_w, bias_n, sinkcol)


def _mlstm_kernel(bi_ref, bf_ref, qpre_ref, kpre_ref, v_ref, mo_ref, gc_ref, gr_ref,
                  cq_ref, ck_ref, cwq_ref, cwk_ref, cbq_ref, cbk_ref, g_ref,
                  c0_ref, n0_ref, m0_ref,
                  o_ref, c_ref, n_ref, m_ref, ext_scr):
    h = pl.program_id(1)
    c = pl.program_id(2)
    L = qpre_ref.shape[1]

    @pl.when(c == 0)
    def _():
        c_ref[...] = c0_ref[...]
        n_ref[...] = n0_ref[...]
        m_ref[...] = m0_ref[...]
        ext_scr[0, 0:8, :] = cq_ref[0]
        ext_scr[1, 0:8, :] = ck_ref[0]

    ext_scr[0, 8:8 + L, :] = qpre_ref[0]
    ext_scr[1, 8:8 + L, :] = kpre_ref[0]

    def conv(idx, w_ref, b_ref):
        y = b_ref[...]
        for j in range(ML_CONV):
            lo = 8 - (ML_CONV - 1) + j
            y = y + ext_scr[idx, lo:lo + L, :] * w_ref[j:j + 1, :]
        return y

    qc = conv(0, cwq_ref, cbq_ref)
    kc = conv(1, cwk_ref, cbk_ref)
    tail_q = ext_scr[0, L:L + 8, :]
    tail_k = ext_scr[1, L:L + 8, :]
    ext_scr[0, 0:8, :] = tail_q
    ext_scr[1, 0:8, :] = tail_k

    q = _silu(qc)
    k = _silu(kc) * (ML_DK ** -0.5)
    qb = q.astype(BF16)
    vb = v_ref[0].astype(BF16)

    b_i = bi_ref[h]
    b_f = bf_ref[h]
    gc = gc_ref[0, 0]
    gr = gr_ref[0, 0]
    i_col = gc[:, 0:1] + b_i
    f_col = _log_sigmoid(gc[:, 1:2] + b_f)
    i_row = gr[0:1, :] + b_i
    f_row = _log_sigmoid(gr[1:2, :] + b_f)

    ti = lax.broadcasted_iota(jnp.int32, (L, L), 0)
    si = lax.broadcasted_iota(jnp.int32, (L, L), 1)
    tri = si <= ti
    bcum_col = jnp.sum(jnp.where(tri, f_row, 0.0), axis=1, keepdims=True)
    bcum_row = jnp.sum(jnp.where(ti <= si, f_col, 0.0), axis=0, keepdims=True)
    b_last = jnp.sum(f_row, axis=1, keepdims=True)
    m_prev = m_ref[0, 0]
    cmat = c_ref[0, 0]
    nrow = n_ref[0, 0]

    dmat = jnp.where(tri, bcum_col - bcum_row + i_row, NEG)
    inter = bcum_col + m_prev
    mt = jnp.maximum(inter, jnp.max(dmat, axis=1, keepdims=True))
    s = lax.dot_general(qb, k.astype(BF16), (((1,), (1,)), ((), ())),
                        preferred_element_type=F32) * jnp.exp(dmat - mt)
    w_in = jnp.exp(inter - mt)
    num = (jnp.dot(s.astype(BF16), vb, preferred_element_type=F32)
           + w_in * jnp.dot(qb, cmat.astype(BF16), preferred_element_type=F32))
    den = jnp.sum(s, axis=1, keepdims=True) + w_in * jnp.sum(q * nrow, axis=1, keepdims=True)
    hh = num / jnp.maximum(jnp.abs(den), jnp.exp(-mt))

    dec_col = b_last - bcum_col + i_col
    dec_row = b_last - bcum_row + i_row
    m_new = jnp.maximum(b_last + m_prev, jnp.max(dec_row, axis=1, keepdims=True))
    wk_col = jnp.exp(dec_col - m_new)
    decay = jnp.exp(b_last + m_prev - m_new)
    kw = wk_col * k
    c_ref[0, 0] = decay * cmat + lax.dot_general(kw.astype(BF16), vb, (((0,), (0,)), ((), ())),
                                                 preferred_element_type=F32)
    n_ref[0, 0] = decay * nrow + jnp.sum(kw, axis=0, keepdims=True)
    m_ref[0, 0] = m_new

    mu = jnp.mean(hh, axis=1, keepdims=True)
    xc = hh - mu
    var = jnp.mean(xc * xc, axis=1, keepdims=True)
    hn = xc * lax.rsqrt(var + LN_EPS) * g_ref[...]
    o_ref[0] = (_sigmoid(mo_ref[0]) * hn).astype(o_ref.dtype)


def _mlstm(p3, gates_col, gates_row, conv8, conv_w, conv_b, norm_g, b_i, b_f, c0, n0, m0, L):
    b, t, _ = p3.shape
    nc = t // L
    smem = pl.BlockSpec(memory_space=pltpu.SMEM)
    qb0 = COL_MQK // ML_DK
    kb0 = (COL_MQK + ML_QK_W) // ML_DK
    vb0 = COL_MV // ML_DV
    ob0 = COL_MO // ML_DV
    kq0 = ML_QK_W // ML_DK
    return pl.pallas_call(
        _mlstm_kernel,
        grid=(b, ML_H, nc),
        in_specs=[smem, smem,
                  pl.BlockSpec((1, L, ML_DK), lambda i, h, c: (i, c, qb0 + h)),
                  pl.BlockSpec((1, L, ML_DK), lambda i, h, c: (i, c, kb0 + h)),
                  pl.BlockSpec((1, L, ML_DV), lambda i, h, c: (i, c, vb0 + h)),
                  pl.BlockSpec((1, L, ML_DV), lambda i, h, c: (i, c, ob0 + h)),
                  pl.BlockSpec((1, 1, L, 2), lambda i, h, c: (i * nc + c, h, 0, 0)),
                  pl.BlockSpec((1, 1, 2, L), lambda i, h, c: (i * nc + c, h, 0, 0)),
                  pl.BlockSpec((1, 8, ML_DK), lambda i, h, c: (i, 0, h)),
                  pl.BlockSpec((1, 8, ML_DK), lambda i, h, c: (i, 0, kq0 + h)),
                  pl.BlockSpec((ML_CONV, ML_DK), lambda i, h, c: (0, h)),
                  pl.BlockSpec((ML_CONV, ML_DK), lambda i, h, c: (0, kq0 + h)),
                  pl.BlockSpec((1, ML_DK), lambda i, h, c: (0, h)),
                  pl.BlockSpec((1, ML_DK), lambda i, h, c: (0, kq0 + h)),
                  pl.BlockSpec((1, ML_DV), lambda i, h, c: (0, h)),
                  pl.BlockSpec((1, 1, ML_DK, ML_DV), lambda i, h, c: (i, h, 0, 0)),
                  pl.BlockSpec((1, 1, 1, ML_DK), lambda i, h, c: (i, h, 0, 0)),
                  pl.BlockSpec((1, 1, 1, 1), lambda i, h, c: (i, h, 0, 0))],
        out_specs=[pl.BlockSpec((1, L, ML_DV), lambda i, h, c: (i, c, h)),
                   pl.BlockSpec((1, 1, ML_DK, ML_DV), lambda i, h, c: (i, h, 0, 0)),
                   pl.BlockSpec((1, 1, 1, ML_DK), lambda i, h, c: (i, h, 0, 0)),
                   pl.BlockSpec((1, 1, 1, 1), lambda i, h, c: (i, h, 0, 0))],
        out_shape=[jax.ShapeDtypeStruct((b, t, ML_V_W), F32),
                   jax.ShapeDtypeStruct((b, ML_H, ML_DK, ML_DV), F32),
                   jax.ShapeDtypeStruct((b, ML_H, 1, ML_DK), F32),
                   jax.ShapeDtypeStruct((b, ML_H, 1, 1), F32)],
        scratch_shapes=[pltpu.VMEM((2, L + 8, ML_DK), F32)],
        compiler_params=_cparams(("parallel", "parallel", "arbitrary")),
        name="mlstm",
    )(b_i, b_f, p3, p3, p3, p3, gates_col, gates_row, conv8, conv8, conv_w, conv_w,
      conv_b, conv_b, norm_g, c0, n0, m0)


def _layer_norm(r, g, b):
    mu = jnp.mean(r, axis=-1, keepdims=True)
    xc = r - mu
    var = jnp.mean(xc * xc, axis=-1, keepdims=True)
    return xc * lax.rsqrt(var + LN_EPS) * g + b


def _mix_kernel(att_ref, hm_ref, ga_ref, gb_ref, x_ref, g1_ref, sc2_ref, sh2_ref,
                wa_ref, wb_ref, wo_ref, lg_ref, lb_ref, x1_ref, h2t_ref):
    ya = jnp.dot(att_ref[...].astype(BF16), wa_ref[...], preferred_element_type=F32)
    yb = jnp.dot(hm_ref[...].astype(BF16), wb_ref[...], preferred_element_type=F32)
    z = _sigmoid(ga_ref[...]) * ya + _sigmoid(gb_ref[...]) * yb
    mix = jnp.dot(z.astype(BF16), wo_ref[...], preferred_element_type=F32)
    shp = x_ref.shape
    r = ALPHA * x_ref[...] + g1_ref[...] * mix.reshape(shp)
    x1 = _layer_norm(r, lg_ref[...], lb_ref[...])
    x1_ref[...] = x1
    h2 = x1 * (1.0 + sc2_ref[...]) + sh2_ref[...]
    h2t_ref[...] = h2.reshape(mix.shape).T.astype(BF16)


def _mix(att2, hm2, p2, x, g1, sc2, sh2, wa, wb, wo, ln_g, ln_b, bb, tt):
    bx, tx, d = x.shape
    tm = bb * tt
    ntok = bx * tx
    tpb = tx // tt
    xmap = lambda i: (i // tpb if bb == 1 else i, i % tpb if bb == 1 else 0, 0)
    mmap = lambda i: (i // tpb if bb == 1 else i, 0, 0)
    wspec = lambda shp: pl.BlockSpec(shp, lambda i: (0, 0), pipeline_mode=pl.Buffered(1))
    return pl.pallas_call(
        _mix_kernel,
        grid=(ntok // tm,),
        in_specs=[pl.BlockSpec((tm, d), lambda i: (i, 0)),
                  pl.BlockSpec((tm, d), lambda i: (i, 0)),
                  pl.BlockSpec((tm, d), lambda i: (i, COL_GA // d)),
                  pl.BlockSpec((tm, d), lambda i: (i, COL_GB // d)),
                  pl.BlockSpec((bb, tt, d), xmap),
                  pl.BlockSpec((bb, 1, d), mmap),
                  pl.BlockSpec((bb, 1, d), mmap),
                  pl.BlockSpec((bb, 1, d), mmap),
                  wspec(wa.shape), wspec(wb.shape), wspec(wo.shape),
                  pl.BlockSpec((1, d), lambda i: (0, 0)),
                  pl.BlockSpec((1, d), lambda i: (0, 0))],
        out_specs=[pl.BlockSpec((bb, tt, d), xmap),
                   pl.BlockSpec((d, tm), lambda i: (0, i))],
        out_shape=[jax.ShapeDtypeStruct((bx, tx, d), F32),
                   jax.ShapeDtypeStruct((d, ntok), BF16)],
        compiler_params=_cparams(("parallel",)),
        name="mix",
    )(att2, hm2, p2, p2, x, g1, sc2, sh2, wa, wb, wo, ln_g, ln_b)


def _top16(val, kio):
    work = val
    rank = jnp.full(val.shape, float(TOPK), F32)
    tops = []
    nkeys = val.shape[0]
    for r in range(TOPK):
        m = jnp.max(work, axis=0, keepdims=True)
        idx = jnp.min(jnp.where(work == m, kio, float(nkeys)), axis=0, keepdims=True)
        hit = kio == idx
        rank = jnp.where(hit, float(r), rank)
        work = jnp.where(hit, -jnp.inf, work)
        tops.append(m)
    return rank, tops


def _peer_select_kernel(h2t_ref, wq_ref, keys_ref, rb_ref, eb_ref, cnt_ref, ea_ref, q_scr):
    tl = h2t_ref.shape[1]
    q_scr[...] = jnp.dot(wq_ref[...], h2t_ref[...], preferred_element_type=F32).astype(BF16)
    kio = lax.broadcasted_iota(jnp.int32, (NK, tl), 0).astype(F32)
    rio = lax.broadcasted_iota(jnp.int32, (TOPK, tl), 0).astype(F32)

    def head(hd, carry):
        qa = q_scr[pl.ds(pl.multiple_of(hd * 2 * NK, NK), NK), :]
        qb = q_scr[pl.ds(pl.multiple_of(hd * 2 * NK + NK, NK), NK), :]
        a = jnp.dot(keys_ref[2 * hd], qa, preferred_element_type=F32)
        b = jnp.dot(keys_ref[2 * hd + 1], qb, preferred_element_type=F32)
        ra, atop = _top16(a, kio)
        rb, btop = _top16(b, kio)
        asort = jnp.concatenate(atop, axis=0)
        cnt = jnp.zeros((TOPK, tl), F32)
        front = asort + btop[0]
        for _ in range(TOPK):
            m = jnp.max(front, axis=0, keepdims=True)
            idx = jnp.min(jnp.where(front == m, rio, float(TOPK)), axis=0, keepdims=True)
            hit = rio == idx
            cnt = cnt + jnp.where(hit, 1.0, 0.0)
            nxt = jnp.full((TOPK, tl), -jnp.inf, F32)
            for cc in range(1, TOPK):
                nxt = jnp.where(cnt == float(cc), btop[cc], nxt)
            front = jnp.where(hit, asort + nxt, front)
        ea_s = jnp.exp(asort - atop[0])
        pref = jnp.zeros((1, tl), F32)
        pbsel = jnp.zeros((TOPK, tl), F32)
        for cc in range(1, TOPK + 1):
            pref = pref + jnp.exp(btop[cc - 1] - btop[0])
            pbsel = jnp.where(cnt == float(cc), pref, pbsel)
        z = jnp.sum(ea_s * pbsel, axis=0, keepdims=True)
        cnti = jnp.zeros((NK, tl), F32)
        for r in range(TOPK):
            cnti = jnp.where(ra == float(r), cnt[r:r + 1, :], cnti)
        rb_ref[hd] = rb
        eb_ref[hd] = jnp.exp(b - btop[0])
        cnt_ref[hd] = cnti
        ea_ref[hd] = jnp.exp(a - atop[0]) / z
        return carry

    lax.fori_loop(0, PEER_H, head, 0)


def _peer_select(h2t, wqt, keys, tl):
    d, ntok = h2t.shape
    out = jax.ShapeDtypeStruct((PEER_H, NK, ntok), F32)
    ospec = pl.BlockSpec((PEER_H, NK, tl), lambda i: (0, 0, i))
    return pl.pallas_call(
        _peer_select_kernel,
        grid=(ntok // tl,),
        in_specs=[pl.BlockSpec((d, tl), lambda i: (0, i)),
                  pl.BlockSpec(wqt.shape, lambda i: (0, 0), pipeline_mode=pl.Buffered(1)),
                  pl.BlockSpec(keys.shape, lambda i: (0, 0, 0))],
        out_specs=[ospec, ospec, ospec, ospec],
        out_shape=[out, out, out, out],
        scratch_shapes=[pltpu.VMEM((wqt.shape[0], tl), BF16)],
        compiler_params=_cparams(("parallel",)),
        name="peer_select",
    )(h2t, wqt, keys)


def _gelu(x):
    return 0.5 * x * (1.0 + lax.erf(x * (2.0 ** -0.5)))


def _peer_dense_kernel(h2t_ref, u_ref, vt_ref, rb_ref, eb_ref, cnt_ref, ea_ref, o_ref, acc_ref):
    e = pl.program_id(1)
    et = u_ref.shape[0]
    nblk = et // NK

    @pl.when(e == 0)
    def _():
        acc_ref[...] = jnp.zeros_like(acc_ref)

    act = jnp.dot(u_ref[...], h2t_ref[...], preferred_element_type=F32)
    parts = []
    for il in range(nblk):
        ig = e * nblk + il
        gate = None
        for hd in range(PEER_H):
            c = cnt_ref[hd, pl.ds(ig, 1), :]
            w = ea_ref[hd, pl.ds(ig, 1), :]
            term = jnp.where(rb_ref[hd] < c, eb_ref[hd] * w, 0.0)
            gate = term if gate is None else gate + term
        parts.append((gate * _gelu(act[il * NK:(il + 1) * NK, :])).astype(BF16))
    mt = jnp.concatenate(parts, axis=0)
    acc_ref[...] += jnp.dot(vt_ref[...], mt, preferred_element_type=F32)

    @pl.when(e == pl.num_programs(1) - 1)
    def _():
        o_ref[...] = acc_ref[...].T


def _peer_dense(h2t, u, vt, rb, eb, cnti, ea, tm, et):
    d, ntok = h2t.shape
    nexp = u.shape[0]
    rspec = pl.BlockSpec((PEER_H, NK, tm), lambda i, e: (0, 0, i))
    return pl.pallas_call(
        _peer_dense_kernel,
        grid=(ntok // tm, nexp // et),
        in_specs=[pl.BlockSpec((d, tm), lambda i, e: (0, i)),
                  pl.BlockSpec((et, d), lambda i, e: (e, 0)),
                  pl.BlockSpec((d, et), lambda i, e: (0, e)),
                  rspec, rspec, rspec, rspec],
        out_specs=pl.BlockSpec((tm, d), lambda i, e: (i, 0)),
        out_shape=jax.ShapeDtypeStruct((ntok, d), F32),
        scratch_shapes=[pltpu.VMEM((d, tm), F32)],
        compiler_params=_cparams(("parallel", "arbitrary")),
        name="peer_dense",
    )(h2t, u, vt, rb, eb, cnti, ea)


def _final_kernel(x1_ref, po_ref, g2_ref, lg_ref, lb_ref, y_ref):
    shp = x1_ref.shape
    r = ALPHA * x1_ref[...] + g2_ref[...] * po_ref[...].reshape(shp)
    y_ref[...] = _layer_norm(r, lg_ref[...], lb_ref[...])


def _final(x1, po, g2, ln_g, ln_b, bb, tt):
    bx, tx, d = x1.shape
    tm = bb * tt
    tpb = tx // tt
    xmap = lambda i: (i // tpb if bb == 1 else i, i % tpb if bb == 1 else 0, 0)
    mmap = lambda i: (i // tpb if bb == 1 else i, 0, 0)
    return pl.pallas_call(
        _final_kernel,
        grid=(bx * tx // tm,),
        in_specs=[pl.BlockSpec((bb, tt, d), xmap),
                  pl.BlockSpec((tm, d), lambda i: (i, 0)),
                  pl.BlockSpec((bb, 1, d), mmap),
                  pl.BlockSpec((1, d), lambda i: (0, 0)),
                  pl.BlockSpec((1, d), lambda i: (0, 0))],
        out_specs=pl.BlockSpec((bb, tt, d), xmap),
        out_shape=jax.ShapeDtypeStruct((bx, tx, d), F32),
        compiler_params=_cparams(("parallel",)),
        name="final_ln",
    )(x1, po, g2, ln_g, ln_b)


def _tile(bx, tx, want):
    if tx >= want:
        return 1, want
    return want // tx, tx


def _layer(x, ada, state, prm):
    bx, tx, d = x.shape
    ntok = bx * tx
    sh1, sc1, g1, sh2, sc2, g2 = [a[:, None, :] for a in jnp.split(ada, 6, axis=-1)]

    bb, tt = _tile(bx, tx, min(512, ntok))
    p2 = _inproj(x, sc1, sh1, prm["w_all"], bb, tt)
    p3 = p2.reshape(bx, tx, IN_PAD)

    if state is None:
        att = _attn_prompt(p3, prm["bias_prompt"], prm["sink_prompt"])
        keep = min(WINDOW, tx)
        k_keep = p3[:, tx - keep:, COL_AK:COL_AK + ATT_KV_W]
        v_keep = p3[:, tx - keep:, COL_AV:COL_AV + ATT_KV_W]
        conv_buf = jnp.zeros((bx, ML_CONV - 1, 2 * ML_QK_W), F32)
        c0 = jnp.zeros((bx, ML_H, ML_DK, ML_DV), F32)
        n0 = jnp.zeros((bx, ML_H, ML_DK), F32)
        m0 = jnp.zeros((bx, ML_H), F32)
    else:
        kwin, vwin, conv_buf, c0, n0, m0 = state
        wc = kwin.shape[1]
        att, k_keep, v_keep = _attn_sample(
            p3, kwin.reshape(bx, wc, ATT_KV_W), vwin.reshape(bx, wc, ATT_KV_W),
            prm["bias_sample_w"], prm["bias_sample_n"], prm["sink_sample"])
    k_keep = k_keep.reshape(bx, -1, ATT_KV, HD)
    v_keep = v_keep.reshape(bx, -1, ATT_KV, HD)

    L = min(ML_CHUNK_MAX, tx)
    nc = tx // L
    gates = p3[:, :, COL_IF:COL_IF + 2 * ML_H].reshape(bx * nc, L, 2, ML_H)
    gates_col = jnp.transpose(gates, (0, 3, 1, 2))
    gates_row = jnp.transpose(gates, (0, 3, 2, 1))
    conv8 = jnp.pad(conv_buf.astype(F32), ((0, 0), (8 - (ML_CONV - 1), 0), (0, 0)))
    hm, c1, n1, m1 = _mlstm(p3, gates_col, gates_row, conv8, prm["conv_w"], prm["conv_b"],
                            prm["norm_g"], prm["b_i"], prm["b_f"], c0.astype(F32),
                            n0.astype(F32).reshape(bx, ML_H, 1, ML_DK),
                            m0.astype(F32).reshape(bx, ML_H, 1, 1), L)
    conv_keep = p3[:, tx - (ML_CONV - 1):, COL_MQK:COL_MQK + 2 * ML_QK_W]
    new_state = (k_keep, v_keep, conv_keep, c1, n1.reshape(bx, ML_H, ML_DK), m1.reshape(bx, ML_H))

    bb, tt = _tile(bx, tx, min(256, ntok))
    x1, h2t = _mix(att.reshape(ntok, ATT_Q_W), hm.reshape(ntok, ML_V_W), p2, x, g1, sc2, sh2,
                   prm["w_a"], prm["w_b"], prm["w_out"], prm["ln1_g"], prm["ln1_b"], bb, tt)

    rb, eb, cnti, ea = _peer_select(h2t, prm["w_pqt"], prm["keys"], min(256, ntok))
    po = _peer_dense(h2t, prm["peer_u"], prm["peer_vt"], rb, eb, cnti, ea, min(512, ntok), 512)

    bb, tt = _tile(bx, tx, min(512, ntok))
    y = _final(x1, po, g2, prm["ln2_g"], prm["ln2_b"], bb, tt)
    return y, new_state


def kernel(x_prompt, x_sample, c_prompt, c_sample, cache_k_win, cache_v_win, state_conv, state_C, state_n, state_m, rel_bias_table, w_ada, b_ada, w_in, b_i, b_f, conv_w, conv_b, ml_norm_g, att_sinks, w_a, w_b, w_out, ln1_g, ln1_b, w_pq, peer_sub_keys, peer_u, peer_v, ln2_g, ln2_b):
    assert w_ada.shape[0] == DEPTH
    d = x_prompt.shape[-1]
    bp = x_prompt.shape[0]
    ts = x_sample.shape[1]
    wc = cache_k_win.shape[2]

    wi = w_in[0]
    o = 0
    seg = {}
    for name, w in (("aq", ATT_Q_W), ("ak", ATT_KV_W), ("av", ATT_KV_W), ("mqk", 2 * ML_QK_W),
                    ("mv", ML_V_W), ("mi", ML_H), ("mf", ML_H), ("mo", ML_V_W), ("ga", d), ("gb", d)):
        seg[name] = wi[:, o:o + w]
        o += w
    w_all = jnp.concatenate(
        [seg["aq"], seg["mqk"], seg["mv"], seg["mo"], seg["ga"], seg["gb"], seg["ak"], seg["av"],
         seg["mi"], seg["mf"], jnp.zeros((d, IN_PAD - COL_IF - 2 * ML_H), F32)], axis=1).astype(BF16)

    sinks = att_sinks[0].astype(F32).reshape(ATT_KV, ATT_G, 1, 1)
    blk = WINDOW
    dist_p = (jnp.arange(blk)[:, None] + blk) - jnp.arange(2 * blk)[None, :]
    dist_s = (jnp.arange(ts)[:, None] + wc) - jnp.arange(wc + ts)[None, :]
    bias_s = _bias_table(rel_bias_table, dist_s)
    prm = {
        "w_all": w_all,
        "bias_prompt": _bias_table(rel_bias_table, dist_p),
        "sink_prompt": jnp.broadcast_to(sinks, (ATT_KV, ATT_G, blk, 1)).reshape(ATT_KV, ATT_G * blk, 1),
        "bias_sample_w": bias_s[:, :, :wc],
        "bias_sample_n": bias_s[:, :, wc:],
        "sink_sample": jnp.broadcast_to(sinks, (ATT_KV, ATT_G, ts, 1)).reshape(ATT_KV, ATT_G * ts, 1),
        "conv_w": conv_w[0], "conv_b": conv_b[0].reshape(1, -1), "norm_g": ml_norm_g[0].reshape(1, -1),
        "b_i": b_i[0], "b_f": b_f[0],
        "w_a": w_a[0].astype(BF16), "w_b": w_b[0].astype(BF16), "w_out": w_out[0].astype(BF16),
        "ln1_g": ln1_g[0].reshape(1, -1), "ln1_b": ln1_b[0].reshape(1, -1),
        "w_pqt": w_pq[0].T.astype(BF16),
        "keys": peer_sub_keys[0].reshape(2 * PEER_H, NK, -1).astype(BF16),
        "peer_u": peer_u[0].astype(BF16),
        "peer_vt": peer_v[0].T.astype(BF16),
        "ln2_g": ln2_g[0].reshape(1, -1), "ln2_b": ln2_b[0].reshape(1, -1),
    }

    ada = _ada(jnp.concatenate([c_prompt, c_sample], axis=0), w_ada[0], b_ada[0])
    yp, sp = _layer(x_prompt, ada[:bp], None, prm)
    ys, ss = _layer(x_sample, ada[bp:],
                    (cache_k_win[0], cache_v_win[0], state_conv[0], state_C[0], state_n[0], state_m[0]), prm)
    stack = lambda s: [a[None] for a in s]
    return (yp, ys, *stack(sp), *stack(ss))
```

```python
import functools
import math

import jax
import jax.numpy as jnp
from jax import lax
from jax.experimental import pallas as pl
from jax.experimental.pallas import tpu as pltpu

F32 = jnp.float32
BF16 = jnp.bfloat16

ATT_HEADS = 32
ATT_KV = 8
ATT_G = ATT_HEADS // ATT_KV
HD = 64
WINDOW = 128
N_BUCKETS = 32
MAX_DISTANCE = 128
ML_H = 4
ML_DK = 256
ML_DV = 512
ML_CONV = 4
PEER_H = 8
NK = 128
TOPK = 16
ATT_Q_W = ATT_HEADS * HD
ATT_KV_W = ATT_KV * HD
ML_QK_W = ML_H * ML_DK
ML_V_W = ML_H * ML_DV
DEPTH = 1
ALPHA = (2.0 * DEPTH) ** 0.25
LN_EPS = 1e-5
NEG = -1e30

COL_AQ = 0
COL_MQK = 2048
COL_MV = 4096
COL_MO = 6144
COL_GA = 8192
COL_GB = 10240
COL_AK = 12288
COL_AV = 12800
COL_IF = 13312
IN_PAD = 13440
IN_TN = 1920

ML_CHUNK_MAX = 256
VMEM_LIMIT = 56 * 1024 * 1024


def _cparams(sem):
    return pltpu.CompilerParams(dimension_semantics=sem, vmem_limit_bytes=VMEM_LIMIT)


def _sigmoid(x):
    return 1.0 / (1.0 + jnp.exp(-x))


def _silu(x):
    return x * _sigmoid(x)


def _log_sigmoid(x):
    return jnp.minimum(x, 0.0) - jnp.log1p(jnp.exp(-jnp.abs(x)))


def _ada_kernel(c_ref, w_ref, b_ref, o_ref):
    c = c_ref[...]
    a = _silu(c).astype(BF16)
    o_ref[...] = jnp.dot(a, w_ref[...].astype(BF16), preferred_element_type=F32) + b_ref[...]


def _ada(c_all, w_ada, b_ada):
    r, d = c_all.shape
    n = w_ada.shape[1]
    tn = 1024
    return pl.pallas_call(
        _ada_kernel,
        grid=(n // tn,),
        in_specs=[pl.BlockSpec((r, d), lambda j: (0, 0)),
                  pl.BlockSpec((d, tn), lambda j: (0, j)),
                  pl.BlockSpec((1, tn), lambda j: (0, j))],
        out_specs=pl.BlockSpec((r, tn), lambda j: (0, j)),
        out_shape=jax.ShapeDtypeStruct((r, n), F32),
        compiler_params=_cparams(("parallel",)),
        name="ada",
    )(c_all, w_ada, b_ada.reshape(1, n))


def _inproj_kernel(x_ref, sc_ref, sh_ref, w_ref, p_ref, h_scr):
    @pl.when(pl.program_id(1) == 0)
    def _():
        h = x_ref[...] * (1.0 + sc_ref[...]) + sh_ref[...]
        h_scr[...] = h.reshape(h_scr.shape).astype(BF16)

    p_ref[...] = jnp.dot(h_scr[...], w_ref[...], preferred_element_type=F32)


def _inproj(x, sc, sh, w_all, bb, tt):
    bx, tx, d = x.shape
    tm = bb * tt
    ntok = bx * tx
    grid = (ntok // tm, IN_PAD // IN_TN)
    tpb = tx // tt
    xmap = lambda i, j: (i // tpb if bb == 1 else i, i % tpb if bb == 1 else 0, 0)
    mmap = lambda i, j: (i // tpb if bb == 1 else i, 0, 0)
    return pl.pallas_call(
        _inproj_kernel,
        grid=grid,
        in_specs=[pl.BlockSpec((bb, tt, d), xmap),
                  pl.BlockSpec((bb, 1, d), mmap),
                  pl.BlockSpec((bb, 1, d), mmap),
                  pl.BlockSpec((d, IN_TN), lambda i, j: (0, j))],
        out_specs=pl.BlockSpec((tm, IN_TN), lambda i, j: (i, j)),
        out_shape=jax.ShapeDtypeStruct((ntok, IN_PAD), F32),
        scratch_shapes=[pltpu.VMEM((tm, d), BF16)],
        compiler_params=_cparams(("parallel", "arbitrary")),
        name="inproj",
    )(x, sc, sh, w_all)


def _t5_bucket(dist):
    max_exact = N_BUCKETS // 2
    d = jnp.maximum(dist, 0)
    dl = jnp.maximum(d, max_exact).astype(F32)
    large = max_exact + (jnp.log(dl / max_exact) / math.log(MAX_DISTANCE / max_exact)
                         * (N_BUCKETS - max_exact)).astype(jnp.int32)
    large = jnp.minimum(large, N_BUCKETS - 1)
    return jnp.where(d < max_exact, d, large)


def _bias_table(rel_table, dist):
    lq, lk = dist.shape
    band = (dist >= 0) & (dist < WINDOW)
    bias = jnp.transpose(rel_table[_t5_bucket(dist)], (2, 0, 1)).astype(F32)
    bias = jnp.where(band[None], bias, NEG)
    return bias.reshape(ATT_KV, ATT_G * lq, lk)


def _softmax_sink_parts(s, sink):
    m = jnp.maximum(jnp.max(s, axis=-1, keepdims=True), sink)
    return m


def _attn_prompt_kernel(q_ref, kp_ref, kc_ref, vp_ref, vc_ref, bias_ref, sink_ref, o_ref):
    n = pl.program_id(1)
    blk = q_ref.shape[1]
    q = q_ref[0]
    kcat = jnp.concatenate([kp_ref[0], kc_ref[0]], axis=0).astype(BF16)
    vcat = jnp.concatenate([vp_ref[0], vc_ref[0]], axis=0).astype(BF16)
    col = lax.broadcasted_iota(jnp.int32, (1, 2 * blk), 1)
    pen = jnp.where((col < blk) & (n == 0), NEG, 0.0)
    outs = []
    for kv in range(ATT_KV):
        qs = jnp.concatenate(
            [q[:, (kv * ATT_G + g) * HD:(kv * ATT_G + g + 1) * HD] for g in range(ATT_G)], axis=0
        ).astype(BF16)
        kh = kcat[:, kv * HD:(kv + 1) * HD]
        vh = vcat[:, kv * HD:(kv + 1) * HD]
        s = lax.dot_general(qs, kh, (((1,), (1,)), ((), ())), preferred_element_type=F32)
        s = s * (HD ** -0.5) + bias_ref[kv] + pen
        sink = sink_ref[kv]
        m = jnp.maximum(jnp.max(s, axis=-1, keepdims=True), sink)
        p = jnp.exp(s - m)
        den = jnp.sum(p, axis=-1, keepdims=True) + jnp.exp(sink - m)
        o = jnp.dot(p.astype(BF16), vh, preferred_element_type=F32) / den
        for g in range(ATT_G):
            outs.append(o[g * blk:(g + 1) * blk])
    o_ref[0] = jnp.concatenate(outs, axis=-1).astype(BF16)


def _attn_prompt(p3, bias, sinkcol):
    b, t, _ = p3.shape
    blk = WINDOW
    nb = t // blk
    kcol, vcol = COL_AK // ATT_KV_W, COL_AV // ATT_KV_W
    return pl.pallas_call(
        _attn_prompt_kernel,
        grid=(b, nb),
        in_specs=[pl.BlockSpec((1, blk, ATT_Q_W), lambda i, n: (i, n, 0)),
                  pl.BlockSpec((1, blk, ATT_KV_W), lambda i, n: (i, jnp.maximum(n - 1, 0), kcol)),
                  pl.BlockSpec((1, blk, ATT_KV_W), lambda i, n: (i, n, kcol)),
                  pl.BlockSpec((1, blk, ATT_KV_W), lambda i, n: (i, jnp.maximum(n - 1, 0), vcol)),
                  pl.BlockSpec((1, blk, ATT_KV_W), lambda i, n: (i, n, vcol)),
                  pl.BlockSpec(bias.shape, lambda i, n: (0, 0, 0)),
                  pl.BlockSpec(sinkcol.shape, lambda i, n: (0, 0, 0))],
        out_specs=pl.BlockSpec((1, blk, ATT_Q_W), lambda i, n: (i, n, 0)),
        out_shape=jax.ShapeDtypeStruct((b, t, ATT_Q_W), BF16),
        compiler_params=_cparams(("parallel", "arbitrary")),
        name="attn_prompt",
    )(p3, p3, p3, p3, p3, bias, sinkcol)


def _attn_sample_kernel(q_ref, kn_ref, vn_ref, kw_ref, vw_ref, bw_ref, bn_ref, sink_ref,
                        o_ref, ko_ref, vo_ref):
    t = q_ref.shape[1]
    wc = kw_ref.shape[1]
    q = q_ref[0]
    kn = kn_ref[0]
    vn = vn_ref[0]
    kw = kw_ref[0]
    vw = vw_ref[0]
    ko_ref[0, 0:wc - t, :] = kw[t:, :]
    ko_ref[0, wc - t:wc, :] = kn
    vo_ref[0, 0:wc - t, :] = vw[t:, :]
    vo_ref[0, wc - t:wc, :] = vn
    knb, vnb, kwb, vwb = kn.astype(BF16), vn.astype(BF16), kw.astype(BF16), vw.astype(BF16)
    outs = []
    nt = (((1,), (1,)), ((), ()))
    for kv in range(ATT_KV):
        qs = jnp.concatenate(
            [q[:, (kv * ATT_G + g) * HD:(kv * ATT_G + g + 1) * HD] for g in range(ATT_G)], axis=0
        ).astype(BF16)
        sl = slice(kv * HD, (kv + 1) * HD)
        sw = lax.dot_general(qs, kwb[:, sl], nt, preferred_element_type=F32) * (HD ** -0.5) + bw_ref[kv]
        sn = lax.dot_general(qs, knb[:, sl], nt, preferred_element_type=F32) * (HD ** -0.5) + bn_ref[kv]
        sink = sink_ref[kv]
        m = jnp.maximum(jnp.maximum(jnp.max(sw, axis=-1, keepdims=True),
                                    jnp.max(sn, axis=-1, keepdims=True)), sink)
        pw = jnp.exp(sw - m)
        pn = jnp.exp(sn - m)
        den = (jnp.sum(pw, axis=-1, keepdims=True) + jnp.sum(pn, axis=-1, keepdims=True)
               + jnp.exp(sink - m))
        o = (jnp.dot(pw.astype(BF16), vwb[:, sl], preferred_element_type=F32)
             + jnp.dot(pn.astype(BF16), vnb[:, sl], preferred_element_type=F32)) / den
        for g in range(ATT_G):
            outs.append(o[g * t:(g + 1) * t])
    o_ref[0] = jnp.concatenate(outs, axis=-1).astype(o_ref.dtype)


def _attn_sample(p3, kwin, vwin, bias_w, bias_n, sinkcol):
    b, t, _ = p3.shape
    wc = kwin.shape[1]
    kcol, vcol = COL_AK // ATT_KV_W, COL_AV // ATT_KV_W
    seq3 = lambda i: (i, 0, 0)
    const3 = lambda i: (0, 0, 0)
    return pl.pallas_call(
        _attn_sample_kernel,
        grid=(b,),
        in_specs=[pl.BlockSpec((1, t, ATT_Q_W), seq3),
                  pl.BlockSpec((1, t, ATT_KV_W), lambda i: (i, 0, kcol)),
                  pl.BlockSpec((1, t, ATT_KV_W), lambda i: (i, 0, vcol)),
                  pl.BlockSpec((1, wc, ATT_KV_W), seq3),
                  pl.BlockSpec((1, wc, ATT_KV_W), seq3),
                  pl.BlockSpec(bias_w.shape, const3),
                  pl.BlockSpec(bias_n.shape, const3),
                  pl.BlockSpec(sinkcol.shape, const3)],
        out_specs=[pl.BlockSpec((1, t, ATT_Q_W), seq3),
                   pl.BlockSpec((1, wc, ATT_KV_W), seq3),
                   pl.BlockSpec((1, wc, ATT_KV_W), seq3)],
        out_shape=[jax.ShapeDtypeStruct((b, t, ATT_Q_W), F32),
                   jax.ShapeDtypeStruct((b, wc, ATT_KV_W), F32),
                   jax.ShapeDtypeStruct((b, wc, ATT_KV_W), F32)],
        compiler_params=_cparams(("parallel",)),
        name="attn_sample",
    )(p3, p3, p3, kwin, vwin, bias_w, bias_n, sinkcol)


def _mlstm_kernel(bi_ref, bf_ref, qpre_ref, kpre_ref, v_ref, mo_ref, gc_ref, gr_ref,
                  cq_ref, ck_ref, cwq_ref, cwk_ref, cbq_ref, cbk_ref, g_ref,
                  c0_ref, n0_ref, m0_ref,
                  o_ref, c_ref, n_ref, m_ref, ext_scr):
    h = pl.program_id(1)
    c = pl.program_id(2)
    L = qpre_ref.shape[1]

    @pl.when(c == 0)
    def _():
        c_ref[...] = c0_ref[...]
        n_ref[...] = n0_ref[...]
        m_ref[...] = m0_ref[...]
        ext_scr[0, 0:8, :] = cq_ref[0]
        ext_scr[1, 0:8, :] = ck_ref[0]

    ext_scr[0, 8:8 + L, :] = qpre_ref[0]
    ext_scr[1, 8:8 + L, :] = kpre_ref[0]

    def conv(idx, w_ref, b_ref):
        y = b_ref[...]
        for j in range(ML_CONV):
            lo = 8 - (ML_CONV - 1) + j
            y = y + ext_scr[idx, lo:lo + L, :] * w_ref[j:j + 1, :]
        return y

    qc = conv(0, cwq_ref, cbq_ref)
    kc = conv(1, cwk_ref, cbk_ref)
    tail_q = ext_scr[0, L:L + 8, :]
    tail_k = ext_scr[1, L:L + 8, :]
    ext_scr[0, 0:8, :] = tail_q
    ext_scr[1, 0:8, :] = tail_k

    q = _silu(qc)
    k = _silu(kc) * (ML_DK ** -0.5)
    qb = q.astype(BF16)
    vb = v_ref[0].astype(BF16)

    b_i = bi_ref[h]
    b_f = bf_ref[h]
    gc = gc_ref[0, 0]
    gr = gr_ref[0, 0]
    i_col = gc[:, 0:1] + b_i
    f_col = _log_sigmoid(gc[:, 1:2] + b_f)
    i_row = gr[0:1, :] + b_i
    f_row = _log_sigmoid(gr[1:2, :] + b_f)

    ti = lax.broadcasted_iota(jnp.int32, (L, L), 0)
    si = lax.broadcasted_iota(jnp.int32, (L, L), 1)
    tri = si <= ti
    bcum_col = jnp.sum(jnp.where(tri, f_row, 0.0), axis=1, keepdims=True)
    bcum_row = jnp.sum(jnp.where(ti <= si, f_col, 0.0), axis=0, keepdims=True)
    b_last = jnp.sum(f_row, axis=1, keepdims=True)
    m_prev = m_ref[0, 0]
    cmat = c_ref[0, 0]
    nrow = n_ref[0, 0]

    dmat = jnp.where(tri, bcum_col - bcum_row + i_row, NEG)
    inter = bcum_col + m_prev
    mt = jnp.maximum(inter, jnp.max(dmat, axis=1, keepdims=True))
    s = lax.dot_general(qb, k.astype(BF16), (((1,), (1,)), ((), ())),
                        preferred_element_type=F32) * jnp.exp(dmat - mt)
    w_in = jnp.exp(inter - mt)
    num = (jnp.dot(s.astype(BF16), vb, preferred_element_type=F32)
           + w_in * jnp.dot(qb, cmat.astype(BF16), preferred_element_type=F32))
    den = jnp.sum(s, axis=1, keepdims=True) + w_in * jnp.sum(q * nrow, axis=1, keepdims=True)
    hh = num / jnp.maximum(jnp.abs(den), jnp.exp(-mt))

    dec_col = b_last - bcum_col + i_col
    dec_row = b_last - bcum_row + i_row
    m_new = jnp.maximum(b_last + m_prev, jnp.max(dec_row, axis=1, keepdims=True))
    wk_col = jnp.exp(dec_col - m_new)
    decay = jnp.exp(b_last + m_prev - m_new)
    kw = wk_col * k
    c_ref[0, 0] = decay * cmat + lax.dot_general(kw.astype(BF16), vb, (((0,), (0,)), ((), ())),
                                                 preferred_element_type=F32)
    n_ref[0, 0] = decay * nrow + jnp.sum(kw, axis=0, keepdims=True)
    m_ref[0, 0] = m_new

    mu = jnp.mean(hh, axis=1, keepdims=True)
    xc = hh - mu
    var = jnp.mean(xc * xc, axis=1, keepdims=True)
    hn = xc * lax.rsqrt(var + LN_EPS) * g_ref[...]
    o_ref[0] = (_sigmoid(mo_ref[0]) * hn).astype(o_ref.dtype)


def _mlstm(p3, gates_col, gates_row, conv8, conv_w, conv_b, norm_g, b_i, b_f, c0, n0, m0, L):
    b, t, _ = p3.shape
    nc = t // L
    smem = pl.BlockSpec(memory_space=pltpu.SMEM)
    qb0 = COL_MQK // ML_DK
    kb0 = (COL_MQK + ML_QK_W) // ML_DK
    vb0 = COL_MV // ML_DV
    ob0 = COL_MO // ML_DV
    kq0 = ML_QK_W // ML_DK
    return pl.pallas_call(
        _mlstm_kernel,
        grid=(b, ML_H, nc),
        in_specs=[smem, smem,
                  pl.BlockSpec((1, L, ML_DK), lambda i, h, c: (i, c, qb0 + h)),
                  pl.BlockSpec((1, L, ML_DK), lambda i, h, c: (i, c, kb0 + h)),
                  pl.BlockSpec((1, L, ML_DV), lambda i, h, c: (i, c, vb0 + h)),
                  pl.BlockSpec((1, L, ML_DV), lambda i, h, c: (i, c, ob0 + h)),
                  pl.BlockSpec((1, 1, L, 2), lambda i, h, c: (i * nc + c, h, 0, 0)),
                  pl.BlockSpec((1, 1, 2, L), lambda i, h, c: (i * nc + c, h, 0, 0)),
                  pl.BlockSpec((1, 8, ML_DK), lambda i, h, c: (i, 0, h)),
                  pl.BlockSpec((1, 8, ML_DK), lambda i, h, c: (i, 0, kq0 + h)),
                  pl.BlockSpec((ML_CONV, ML_DK), lambda i, h, c: (0, h)),
                  pl.BlockSpec((ML_CONV, ML_DK), lambda i, h, c: (0, kq0 + h)),
                  pl.BlockSpec((1, ML_DK), lambda i, h, c: (0, h)),
                  pl.BlockSpec((1, ML_DK), lambda i, h, c: (0, kq0 + h)),
                  pl.BlockSpec((1, ML_DV), lambda i, h, c: (0, h)),
                  pl.BlockSpec((1, 1, ML_DK, ML_DV), lambda i, h, c: (i, h, 0, 0)),
                  pl.BlockSpec((1, 1, 1, ML_DK), lambda i, h, c: (i, h, 0, 0)),
                  pl.BlockSpec((1, 1, 1, 1), lambda i, h, c: (i, h, 0, 0))],
        out_specs=[pl.BlockSpec((1, L, ML_DV), lambda i, h, c: (i, c, h)),
                   pl.BlockSpec((1, 1, ML_DK, ML_DV), lambda i, h, c: (i, h, 0, 0)),
                   pl.BlockSpec((1, 1, 1, ML_DK), lambda i, h, c: (i, h, 0, 0)),
                   pl.BlockSpec((1, 1, 1, 1), lambda i, h, c: (i, h, 0, 0))],
        out_shape=[jax.ShapeDtypeStruct((b, t, ML_V_W), F32),
                   jax.ShapeDtypeStruct((b, ML_H, ML_DK, ML_DV), F32),
                   jax.ShapeDtypeStruct((b, ML_H, 1, ML_DK), F32),
                   jax.ShapeDtypeStruct((b, ML_H, 1, 1), F32)],
        scratch_shapes=[pltpu.VMEM((2, L + 8, ML_DK), F32)],
        compiler_params=_cparams(("parallel", "parallel", "arbitrary")),
        name="mlstm",
    )(b_i, b_f, p3, p3, p3, p3, gates_col, gates_row, conv8, conv8, conv_w, conv_w,
      conv_b, conv_b, norm_g, c0, n0, m0)


def _layer_norm(r, g, b):
    mu = jnp.mean(r, axis=-1, keepdims=True)
    xc = r - mu
    var = jnp.mean(xc * xc, axis=-1, keepdims=True)
    return xc * lax.rsqrt(var + LN_EPS) * g + b


def _mix_kernel(att_ref, hm_ref, ga_ref, gb_ref, x_ref, g1_ref, sc2_ref, sh2_ref,
                wa_ref, wb_ref, wo_ref, lg_ref, lb_ref, x1_ref, h2t_ref):
    ya = jnp.dot(att_ref[...].astype(BF16), wa_ref[...], preferred_element_type=F32)
    yb = jnp.dot(hm_ref[...].astype(BF16), wb_ref[...], preferred_element_type=F32)
    z = _sigmoid(ga_ref[...]) * ya + _sigmoid(gb_ref[...]) * yb
    mix = jnp.dot(z.astype(BF16), wo_ref[...], preferred_element_type=F32)
    shp = x_ref.shape
    r = ALPHA * x_ref[...] + g1_ref[...] * mix.reshape(shp)
    x1 = _layer_norm(r, lg_ref[...], lb_ref[...])
    x1_ref[...] = x1
    h2 = x1 * (1.0 + sc2_ref[...]) + sh2_ref[...]
    h2t_ref[...] = h2.reshape(mix.shape).T.astype(BF16)


def _mix(att2, hm2, p2, x, g1, sc2, sh2, wa, wb, wo, ln_g, ln_b, bb, tt):
    bx, tx, d = x.shape
    tm = bb * tt
    ntok = bx * tx
    tpb = tx // tt
    xmap = lambda i: (i // tpb if bb == 1 else i, i % tpb if bb == 1 else 0, 0)
    mmap = lambda i: (i // tpb if bb == 1 else i, 0, 0)
    wspec = lambda shp: pl.BlockSpec(shp, lambda i: (0, 0), pipeline_mode=pl.Buffered(1))
    return pl.pallas_call(
        _mix_kernel,
        grid=(ntok // tm,),
        in_specs=[pl.BlockSpec((tm, d), lambda i: (i, 0)),
                  pl.BlockSpec((tm, d), lambda i: (i, 0)),
                  pl.BlockSpec((tm, d), lambda i: (i, COL_GA // d)),
                  pl.BlockSpec((tm, d), lambda i: (i, COL_GB // d)),
                  pl.BlockSpec((bb, tt, d), xmap),
                  pl.BlockSpec((bb, 1, d), mmap),
                  pl.BlockSpec((bb, 1, d), mmap),
                  pl.BlockSpec((bb, 1, d), mmap),
                  wspec(wa.shape), wspec(wb.shape), wspec(wo.shape),
                  pl.BlockSpec((1, d), lambda i: (0, 0)),
                  pl.BlockSpec((1, d), lambda i: (0, 0))],
        out_specs=[pl.BlockSpec((bb, tt, d), xmap),
                   pl.BlockSpec((d, tm), lambda i: (0, i))],
        out_shape=[jax.ShapeDtypeStruct((bx, tx, d), F32),
                   jax.ShapeDtypeStruct((d, ntok), BF16)],
        compiler_params=_cparams(("parallel",)),
        name="mix",
    )(att2, hm2, p2, p2, x, g1, sc2, sh2, wa, wb, wo, ln_g, ln_b)


def _top16(val, kio):
    work = val
    rank = jnp.full(val.shape, float(TOPK), F32)
    tops = []
    nkeys = val.shape[0]
    for r in range(TOPK):
        m = jnp.max(work, axis=0, keepdims=True)
        idx = jnp.min(jnp.where(work == m, kio, float(nkeys)), axis=0, keepdims=True)
        hit = kio == idx
        rank = jnp.where(hit, float(r), rank)
        work = jnp.where(hit, -jnp.inf, work)
        tops.append(m)
    return rank, tops


def _peer_select_kernel(h2t_ref, wq_ref, keys_ref, rb_ref, eb_ref, cnt_ref, ea_ref, q_scr):
    tl = h2t_ref.shape[1]
    q_scr[...] = jnp.dot(wq_ref[...], h2t_ref[...], preferred_element_type=F32).astype(BF16)
    kio = lax.broadcasted_iota(jnp.int32, (NK, tl), 0).astype(F32)
    rio = lax.broadcasted_iota(jnp.int32, (TOPK, tl), 0).astype(F32)

    def head(hd, carry):
        qa = q_scr[pl.ds(pl.multiple_of(hd * 2 * NK, NK), NK), :]
        qb = q_scr[pl.ds(pl.multiple_of(hd * 2 * NK + NK, NK), NK), :]
        a = jnp.dot(keys_ref[2 * hd], qa, preferred_element_type=F32)
        b = jnp.dot(keys_ref[2 * hd + 1], qb, preferred_element_type=F32)
        ra, atop = _top16(a, kio)
        rb, btop = _top16(b, kio)
        asort = jnp.concatenate(atop, axis=0)
        cnt = jnp.zeros((TOPK, tl), F32)
        front = asort + btop[0]
        for _ in range(TOPK):
            m = jnp.max(front, axis=0, keepdims=True)
            idx = jnp.min(jnp.where(front == m, rio, float(TOPK)), axis=0, keepdims=True)
            hit = rio == idx
            cnt = cnt + jnp.where(hit, 1.0, 0.0)
            nxt = jnp.full((TOPK, tl), -jnp.inf, F32)
            for cc in range(1, TOPK):
                nxt = jnp.where(cnt == float(cc), btop[cc], nxt)
            front = jnp.where(hit, asort + nxt, front)
        ea_s = jnp.exp(asort - atop[0])
        pref = jnp.zeros((1, tl), F32)
        pbsel = jnp.zeros((TOPK, tl), F32)
        for cc in range(1, TOPK + 1):
            pref = pref + jnp.exp(btop[cc - 1] - btop[0])
            pbsel = jnp.where(cnt == float(cc), pref, pbsel)
        z = jnp.sum(ea_s * pbsel, axis=0, keepdims=True)
        cnti = jnp.zeros((NK, tl), F32)
        for r in range(TOPK):
            cnti = jnp.where(ra == float(r), cnt[r:r + 1, :], cnti)
        rb_ref[hd] = rb.astype(rb_ref.dtype)
        eb_ref[hd] = jnp.exp(b - btop[0]).astype(eb_ref.dtype)
        cnt_ref[hd] = cnti
        ea_ref[hd] = jnp.exp(a - atop[0]) * (0.5 / z)
        return carry

    lax.fori_loop(0, PEER_H, head, 0)


def _peer_select(h2t, wqt, keys, tl):
    d, ntok = h2t.shape
    out = jax.ShapeDtypeStruct((PEER_H, NK, ntok), F32)
    outb = jax.ShapeDtypeStruct((PEER_H, NK, ntok), jnp.bfloat16)
    ospec = pl.BlockSpec((PEER_H, NK, tl), lambda i: (0, 0, i))
    return pl.pallas_call(
        _peer_select_kernel,
        grid=(ntok // tl,),
        in_specs=[pl.BlockSpec((d, tl), lambda i: (0, i)),
                  pl.BlockSpec(wqt.shape, lambda i: (0, 0), pipeline_mode=pl.Buffered(1)),
                  pl.BlockSpec(keys.shape, lambda i: (0, 0, 0))],
        out_specs=[ospec, ospec, ospec, ospec],
        out_shape=[outb, outb, out, out],
        scratch_shapes=[pltpu.VMEM((wqt.shape[0], tl), BF16)],
        compiler_params=_cparams(("parallel",)),
        name="peer_select",
    )(h2t, wqt, keys)


def _peer_dense_kernel(h2t_ref, u_ref, vt_ref, rb_ref, eb_ref, cnt_ref, ea_ref, o_ref,
                       act0_ref, act1_ref, mt_ref, *, n_e):
    s = pl.program_id(0)
    e_prev = jnp.maximum(s - 1, 0) % n_e
    half = u_ref.shape[0] // 2
    zero = jnp.zeros((), rb_ref.dtype)

    @pl.when(s == 0)
    def _():
        act1_ref[...] = jnp.zeros_like(act1_ref)

    @pl.when(e_prev == 0)
    def _():
        o_ref[...] = jnp.zeros_like(o_ref)

    tm = h2t_ref.shape[1]
    pk = rb_ref.shape[2]

    def row(ref, hd, il):
        return jnp.broadcast_to(ref[hd, il:il + 1, :], (pk, tm)).astype(rb_ref.dtype)[None]

    def gated(old_ref, hf):
        for il in range(hf * half // NK, (hf + 1) * half // NK):
            gate = None
            for hd in range(PEER_H):
                term = jnp.where(rb_ref[hd] < row(cnt_ref, hd, il), eb_ref[hd] * row(ea_ref, hd, il), zero)
                gate = term if gate is None else gate + term
            rows = slice(il * NK, (il + 1) * NK)
            a = old_ref[rows, :]
            half_gelu2 = a * (1.0 + lax.erf(a * (2.0 ** -0.5)))
            mt_ref[rows, :] = (gate.reshape(NK, tm).astype(F32) * half_gelu2).astype(BF16)

    def phases(new_ref, old_ref):
        new_ref[...] = jnp.dot(u_ref[...], h2t_ref[...], preferred_element_type=F32)
        for hf in range(2):
            rows = slice(hf * half, (hf + 1) * half)
            gated(old_ref, hf)
            o_ref[...] += jnp.dot(vt_ref[:, rows], mt_ref[rows, :], preferred_element_type=F32)

    @pl.when(s % 2 == 0)
    def _():
        phases(act0_ref, act1_ref)

    @pl.when(s % 2 == 1)
    def _():
        phases(act1_ref, act0_ref)


def _peer_dense(h2t, u, vt, rb, eb, cnti, ea, tm):
    d, ntok = h2t.shape
    et = 8 * NK
    n_e = u.shape[0] // et
    n_items = (ntok // tm) * n_e
    item_a = lambda s: jnp.minimum(s, n_items - 1)
    item_b = lambda s: jnp.maximum(s - 1, 0)
    pk = 16
    rb = rb.reshape(PEER_H, NK // pk, pk, ntok)
    eb = eb.reshape(PEER_H, NK // pk, pk, ntok)
    rspec = pl.BlockSpec((PEER_H, NK // pk, pk, tm), lambda s: (0, 0, 0, item_b(s) // n_e))
    cspec = pl.BlockSpec((PEER_H, 8, tm), lambda s: (0, item_b(s) % n_e, item_b(s) // n_e))
    return pl.pallas_call(
        functools.partial(_peer_dense_kernel, n_e=n_e),
        grid=(n_items + 1,),
        in_specs=[pl.BlockSpec((d, tm), lambda s: (0, item_a(s) // n_e)),
                  pl.BlockSpec((et, d), lambda s: (item_a(s) % n_e, 0)),
                  pl.BlockSpec((d, et), lambda s: (0, item_b(s) % n_e)),
                  rspec, rspec, cspec, cspec],
        out_specs=pl.BlockSpec((d, tm), lambda s: (0, item_b(s) // n_e)),
        out_shape=jax.ShapeDtypeStruct((d, ntok), F32),
        scratch_shapes=[pltpu.VMEM((et, tm), F32), pltpu.VMEM((et, tm), F32),
                        pltpu.VMEM((et, tm), BF16)],
        compiler_params=_cparams(("arbitrary",)),
        name="peer_dense",
    )(h2t, u, vt, rb, eb, cnti, ea)


def _final_kernel(x1_ref, po_ref, g2_ref, lg_ref, lb_ref, y_ref):
    shp = x1_ref.shape
    r = ALPHA * x1_ref[...] + g2_ref[...] * po_ref[...].T.reshape(shp)
    y_ref[...] = _layer_norm(r, lg_ref[...], lb_ref[...])


def _final(x1, po, g2, ln_g, ln_b, bb, tt):
    bx, tx, d = x1.shape
    tm = bb * tt
    tpb = tx // tt
    xmap = lambda i: (i // tpb if bb == 1 else i, i % tpb if bb == 1 else 0, 0)
    mmap = lambda i: (i // tpb if bb == 1 else i, 0, 0)
    return pl.pallas_call(
        _final_kernel,
        grid=(bx * tx // tm,),
        in_specs=[pl.BlockSpec((bb, tt, d), xmap),
                  pl.BlockSpec((d, tm), lambda i: (0, i)),
                  pl.BlockSpec((bb, 1, d), mmap),
                  pl.BlockSpec((1, d), lambda i: (0, 0)),
                  pl.BlockSpec((1, d), lambda i: (0, 0))],
        out_specs=pl.BlockSpec((bb, tt, d), xmap),
        out_shape=jax.ShapeDtypeStruct((bx, tx, d), F32),
        compiler_params=_cparams(("parallel",)),
        name="final_ln",
    )(x1, po, g2, ln_g, ln_b)


def _tile(bx, tx, want):
    if tx >= want:
        return 1, want
    return want // tx, tx


def _layer(x, ada, state, prm):
    bx, tx, d = x.shape
    ntok = bx * tx
    sh1, sc1, g1, sh2, sc2, g2 = [a[:, None, :] for a in jnp.split(ada, 6, axis=-1)]

    bb, tt = _tile(bx, tx, min(512, ntok))
    p2 = _inproj(x, sc1, sh1, prm["w_all"], bb, tt)
    p3 = p2.reshape(bx, tx, IN_PAD)

    if state is None:
        att = _attn_prompt(p3, prm["bias_prompt"], prm["sink_prompt"])
        keep = min(WINDOW, tx)
        k_keep = p3[:, tx - keep:, COL_AK:COL_AK + ATT_KV_W]
        v_keep = p3[:, tx - keep:, COL_AV:COL_AV + ATT_KV_W]
        conv_buf = jnp.zeros((bx, ML_CONV - 1, 2 * ML_QK_W), F32)
        c0 = jnp.zeros((bx, ML_H, ML_DK, ML_DV), F32)
        n0 = jnp.zeros((bx, ML_H, ML_DK), F32)
        m0 = jnp.zeros((bx, ML_H), F32)
    else:
        kwin, vwin, conv_buf, c0, n0, m0 = state
        wc = kwin.shape[1]
        att, k_keep, v_keep = _attn_sample(
            p3, kwin.reshape(bx, wc, ATT_KV_W), vwin.reshape(bx, wc, ATT_KV_W),
            prm["bias_sample_w"], prm["bias_sample_n"], prm["sink_sample"])
    k_keep = k_keep.reshape(bx, -1, ATT_KV, HD)
    v_keep = v_keep.reshape(bx, -1, ATT_KV, HD)

    L = min(ML_CHUNK_MAX, tx)
    nc = tx // L
    gates = p3[:, :, COL_IF:COL_IF + 2 * ML_H].reshape(bx * nc, L, 2, ML_H)
    gates_col = jnp.transpose(gates, (0, 3, 1, 2))
    gates_row = jnp.transpose(gates, (0, 3, 2, 1))
    conv8 = jnp.pad(conv_buf.astype(F32), ((0, 0), (8 - (ML_CONV - 1), 0), (0, 0)))
    hm, c1, n1, m1 = _mlstm(p3, gates_col, gates_row, conv8, prm["conv_w"], prm["conv_b"],
                            prm["norm_g"], prm["b_i"], prm["b_f"], c0.astype(F32),
                            n0.astype(F32).reshape(bx, ML_H, 1, ML_DK),
                            m0.astype(F32).reshape(bx, ML_H, 1, 1), L)
    conv_keep = p3[:, tx - (ML_CONV - 1):, COL_MQK:COL_MQK + 2 * ML_QK_W]
    new_state = (k_keep, v_keep, conv_keep, c1, n1.reshape(bx, ML_H, ML_DK), m1.reshape(bx, ML_H))

    bb, tt = _tile(bx, tx, min(256, ntok))
    x1, h2t = _mix(att.reshape(ntok, ATT_Q_W), hm.reshape(ntok, ML_V_W), p2, x, g1, sc2, sh2,
                   prm["w_a"], prm["w_b"], prm["w_out"], prm["ln1_g"], prm["ln1_b"], bb, tt)

    rb, eb, cnti, ea = _peer_select(h2t, prm["w_pqt"], prm["keys"], min(256, ntok))
    po = _peer_dense(h2t, prm["peer_u"], prm["peer_vt"], rb, eb, cnti, ea, min(512, ntok))

    bb, tt = _tile(bx, tx, min(512, ntok))
    y = _final(x1, po, g2, prm["ln2_g"], prm["ln2_b"], bb, tt)
    return y, new_state


def kernel(x_prompt, x_sample, c_prompt, c_sample, cache_k_win, cache_v_win, state_conv, state_C, state_n, state_m, rel_bias_table, w_ada, b_ada, w_in, b_i, b_f, conv_w, conv_b, ml_norm_g, att_sinks, w_a, w_b, w_out, ln1_g, ln1_b, w_pq, peer_sub_keys, peer_u, peer_v, ln2_g, ln2_b):
    assert w_ada.shape[0] == DEPTH
    d = x_prompt.shape[-1]
    bp = x_prompt.shape[0]
    ts = x_sample.shape[1]
    wc = cache_k_win.shape[2]

    wi = w_in[0]
    o = 0
    seg = {}
    for name, w in (("aq", ATT_Q_W), ("ak", ATT_KV_W), ("av", ATT_KV_W), ("mqk", 2 * ML_QK_W),
                    ("mv", ML_V_W), ("mi", ML_H), ("mf", ML_H), ("mo", ML_V_W), ("ga", d), ("gb", d)):
        seg[name] = wi[:, o:o + w]
        o += w
    w_all = jnp.concatenate(
        [seg["aq"], seg["mqk"], seg["mv"], seg["mo"], seg["ga"], seg["gb"], seg["ak"], seg["av"],
         seg["mi"], seg["mf"], jnp.zeros((d, IN_PAD - COL_IF - 2 * ML_H), F32)], axis=1).astype(BF16)

    sinks = att_sinks[0].astype(F32).reshape(ATT_KV, ATT_G, 1, 1)
    blk = WINDOW
    dist_p = (jnp.arange(blk)[:, None] + blk) - jnp.arange(2 * blk)[None, :]
    dist_s = (jnp.arange(ts)[:, None] + wc) - jnp.arange(wc + ts)[None, :]
    bias_s = _bias_table(rel_bias_table, dist_s)
    prm = {
        "w_all": w_all,
        "bias_prompt": _bias_table(rel_bias_table, dist_p),
        "sink_prompt": jnp.broadcast_to(sinks, (ATT_KV, ATT_G, blk, 1)).reshape(ATT_KV, ATT_G * blk, 1),
        "bias_sample_w": bias_s[:, :, :wc],
        "bias_sample_n": bias_s[:, :, wc:],
        "sink_sample": jnp.broadcast_to(sinks, (ATT_KV, ATT_G, ts, 1)).reshape(ATT_KV, ATT_G * ts, 1),
        "conv_w": conv_w[0], "conv_b": conv_b[0].reshape(1, -1), "norm_g": ml_norm_g[0].reshape(1, -1),
        "b_i": b_i[0], "b_f": b_f[0],
        "w_a": w_a[0].astype(BF16), "w_b": w_b[0].astype(BF16), "w_out": w_out[0].astype(BF16),
        "ln1_g": ln1_g[0].reshape(1, -1), "ln1_b": ln1_b[0].reshape(1, -1),
        "w_pqt": w_pq[0].T.astype(BF16),
        "keys": peer_sub_keys[0].reshape(2 * PEER_H, NK, -1).astype(BF16),
        "peer_u": peer_u[0].astype(BF16),
        "peer_vt": peer_v[0].T.astype(BF16),
        "ln2_g": ln2_g[0].reshape(1, -1), "ln2_b": ln2_b[0].reshape(1, -1),
    }

    ada = _ada(jnp.concatenate([c_prompt, c_sample], axis=0), w_ada[0], b_ada[0])
    yp, sp = _layer(x_prompt, ada[:bp], None, prm)
    ys, ss = _layer(x_sample, ada[bp:],
                    (cache_k_win[0], cache_v_win[0], state_conv[0], state_C[0], state_n[0], state_m[0]), prm)
    stack = lambda s: [a[None] for a in s]
    return (yp, ys, *stack(sp), *stack(ss))
```

```python
import functools
import math

import jax
import jax.numpy as jnp
from jax import lax
from jax.experimental import pallas as pl
from jax.experimental.pallas import tpu as pltpu

F32 = jnp.float32
BF16 = jnp.bfloat16

ATT_HEADS = 32
ATT_KV = 8
ATT_G = ATT_HEADS // ATT_KV
HD = 64
WINDOW = 128
N_BUCKETS = 32
MAX_DISTANCE = 128
ML_H = 4
ML_DK = 256
ML_DV = 512
ML_CONV = 4
PEER_H = 8
NK = 128
TOPK = 16
ATT_Q_W = ATT_HEADS * HD
ATT_KV_W = ATT_KV * HD
ML_QK_W = ML_H * ML_DK
ML_V_W = ML_H * ML_DV
DEPTH = 1
ALPHA = (2.0 * DEPTH) ** 0.25
LN_EPS = 1e-5
NEG = -1e30

COL_AQ = 0
COL_AK = 2048
COL_AV = 2560
COL_MQK = 3072
COL_MV = 5120
IN_W1 = 7168
COL_MO = 7168
COL_GA = 9216
COL_GB = 11264
IN_MAIN = 13312
IN_TN = 1024
IF_PAD = 128

ML_CHUNK_MAX = 256
VMEM_LIMIT = 56 * 1024 * 1024


def _cparams(sem):
    return pltpu.CompilerParams(dimension_semantics=sem, vmem_limit_bytes=VMEM_LIMIT)


def _sigmoid(x):
    return 1.0 / (1.0 + jnp.exp(-x))


def _silu(x):
    return x * _sigmoid(x)


def _log_sigmoid(x):
    return jnp.minimum(x, 0.0) - jnp.log1p(jnp.exp(-jnp.abs(x)))


def _ada_kernel(c_ref, w_ref, b_ref, o_ref):
    c = c_ref[...]
    a = _silu(c).astype(BF16)
    o_ref[...] = jnp.dot(a, w_ref[...].astype(BF16), preferred_element_type=F32) + b_ref[...]


def _ada(c_all, w_ada, b_ada):
    r, d = c_all.shape
    n = w_ada.shape[1]
    tn = 1024
    return pl.pallas_call(
        _ada_kernel,
        grid=(n // tn,),
        in_specs=[pl.BlockSpec((r, d), lambda j: (0, 0)),
                  pl.BlockSpec((d, tn), lambda j: (0, j)),
                  pl.BlockSpec((1, tn), lambda j: (0, j))],
        out_specs=pl.BlockSpec((r, tn), lambda j: (0, j)),
        out_shape=jax.ShapeDtypeStruct((r, n), F32),
        compiler_params=_cparams(("parallel",)),
        name="ada",
    )(c_all, w_ada, b_ada.reshape(1, n))


def _inproj_kernel(x_ref, sc_ref, sh_ref, w1_ref, w2_ref, wif_ref, p_ref, if_ref, h_scr, *, n1):
    j = pl.program_id(1)

    @pl.when(j == 0)
    def _():
        h = x_ref[...] * (1.0 + sc_ref[...]) + sh_ref[...]
        hb = h.reshape(h_scr.shape).astype(BF16)
        h_scr[...] = hb
        if_ref[...] = jnp.dot(hb, wif_ref[...], preferred_element_type=F32)

    @pl.when(j < n1)
    def _():
        p_ref[...] = jnp.dot(h_scr[...], w1_ref[...], preferred_element_type=F32).astype(p_ref.dtype)

    @pl.when(j >= n1)
    def _():
        p_ref[...] = jnp.dot(h_scr[...], w2_ref[...], preferred_element_type=F32).astype(p_ref.dtype)


def _inproj(x, sc, sh, w1, w2, wif, bb, tt):
    bx, tx, d = x.shape
    tm = bb * tt
    ntok = bx * tx
    n1 = w1.shape[1] // IN_TN
    n2 = w2.shape[1] // IN_TN
    tpb = tx // tt
    xmap = lambda i, j: (i // tpb if bb == 1 else i, i % tpb if bb == 1 else 0, 0)
    mmap = lambda i, j: (i // tpb if bb == 1 else i, 0, 0)
    return pl.pallas_call(
        functools.partial(_inproj_kernel, n1=n1),
        grid=(ntok // tm, n1 + n2),
        in_specs=[pl.BlockSpec((bb, tt, d), xmap),
                  pl.BlockSpec((bb, 1, d), mmap),
                  pl.BlockSpec((bb, 1, d), mmap),
                  pl.BlockSpec((d, IN_TN), lambda i, j: (0, jnp.minimum(j, n1 - 1))),
                  pl.BlockSpec((d, IN_TN), lambda i, j: (0, jnp.maximum(j - n1, 0))),
                  pl.BlockSpec((d, IF_PAD), lambda i, j: (0, 0))],
        out_specs=[pl.BlockSpec((tm, IN_TN), lambda i, j: (i, j)),
                   pl.BlockSpec((tm, IF_PAD), lambda i, j: (i, 0))],
        out_shape=[jax.ShapeDtypeStruct((ntok, IN_MAIN), BF16),
                   jax.ShapeDtypeStruct((ntok, IF_PAD), F32)],
        scratch_shapes=[pltpu.VMEM((tm, d), BF16)],
        compiler_params=_cparams(("parallel", "arbitrary")),
        name="inproj",
    )(x, sc, sh, w1, w2, wif)


def _t5_bucket(dist):
    max_exact = N_BUCKETS // 2
    d = jnp.maximum(dist, 0)
    dl = jnp.maximum(d, max_exact).astype(F32)
    large = max_exact + (jnp.log(dl / max_exact) / math.log(MAX_DISTANCE / max_exact)
                         * (N_BUCKETS - max_exact)).astype(jnp.int32)
    large = jnp.minimum(large, N_BUCKETS - 1)
    return jnp.where(d < max_exact, d, large)


def _bias_table(rel_table, dist):
    lq, lk = dist.shape
    band = (dist >= 0) & (dist < WINDOW)
    bias = jnp.transpose(rel_table[_t5_bucket(dist)], (2, 0, 1)).astype(F32)
    bias = jnp.where(band[None], bias, NEG)
    return bias.reshape(ATT_KV, ATT_G * lq, lk)


def _bias_table_t(rel_table, dist):
    lq, lk = dist.shape
    bucket = _t5_bucket(dist).T.reshape(1, lk * lq)
    onehot = (bucket == jnp.arange(N_BUCKETS)[:, None]).astype(F32)
    bias = jnp.dot(rel_table.T.astype(F32), onehot, precision=lax.Precision.HIGHEST)
    band = ((dist >= 0) & (dist < WINDOW)).T.reshape(1, lk * lq)
    return jnp.where(band, bias, NEG).reshape(-1, lk, lq)


def _attn_prompt_kernel(q_ref, kp_ref, kc_ref, vp_ref, vc_ref, bias_ref, sink_ref, o_ref):
    qt = (q_ref[0] * (HD ** -0.5)).T.astype(BF16)
    kcat = jnp.concatenate([kp_ref[0], kc_ref[0]], axis=0).astype(BF16)
    vt = jnp.concatenate([vp_ref[0], vc_ref[0]], axis=0).T.astype(BF16)

    def scores(kv):
        kh = kcat[:, kv * HD:(kv + 1) * HD]
        return [jnp.dot(kh, qt[h * HD:(h + 1) * HD, :], preferred_element_type=F32) + bias_ref[0, h]
                for h in range(kv * ATT_G, (kv + 1) * ATT_G)]

    def finish(kv, sts):
        vth = vt[kv * HD:(kv + 1) * HD, :]
        ps, dens = [], []
        for g, st in enumerate(sts):
            sink = sink_ref[kv * ATT_G + g]
            m = jnp.maximum(jnp.max(st, axis=0, keepdims=True), sink)
            p = jnp.exp(st - m)
            dens.append(jnp.sum(p, axis=0, keepdims=True) + jnp.exp(sink - m))
            ps.append(p.astype(BF16))
        outs = [jnp.dot(vth, p, preferred_element_type=F32) / den for p, den in zip(ps, dens)]
        for g in range(0, ATT_G, 2):
            h = kv * ATT_G + g
            o_ref[0, :, h * HD:(h + 2) * HD] = jnp.concatenate(outs[g:g + 2], axis=0).T.astype(o_ref.dtype)

    nxt = scores(0)
    for kv in range(ATT_KV):
        cur = nxt
        if kv + 1 < ATT_KV:
            nxt = scores(kv + 1)
        finish(kv, cur)


def _attn_prompt(p3, bias2, sinkrow):
    b, t, _ = p3.shape
    blk = WINDOW
    nb = t // blk
    kcol, vcol = COL_AK // ATT_KV_W, COL_AV // ATT_KV_W
    return pl.pallas_call(
        _attn_prompt_kernel,
        grid=(b, nb),
        in_specs=[pl.BlockSpec((1, blk, ATT_Q_W), lambda i, n: (i, n, 0)),
                  pl.BlockSpec((1, blk, ATT_KV_W), lambda i, n: (i, jnp.maximum(n - 1, 0), kcol)),
                  pl.BlockSpec((1, blk, ATT_KV_W), lambda i, n: (i, n, kcol)),
                  pl.BlockSpec((1, blk, ATT_KV_W), lambda i, n: (i, jnp.maximum(n - 1, 0), vcol)),
                  pl.BlockSpec((1, blk, ATT_KV_W), lambda i, n: (i, n, vcol)),
                  pl.BlockSpec((1,) + bias2.shape[1:], lambda i, n: (jnp.minimum(n, 1), 0, 0, 0)),
                  pl.BlockSpec(sinkrow.shape, lambda i, n: (0, 0, 0))],
        out_specs=pl.BlockSpec((1, blk, ATT_Q_W), lambda i, n: (i, n, 0)),
        out_shape=jax.ShapeDtypeStruct((b, t, ATT_Q_W), BF16),
        compiler_params=_cparams(("parallel", "arbitrary")),
        name="attn_prompt",
    )(p3, p3, p3, p3, p3, bias2, sinkrow)


def _attn_sample_kernel(q_ref, kn_ref, vn_ref, kw_ref, vw_ref, bw_ref, bn_ref, sink_ref,
                        o_ref, ko_ref, vo_ref):
    bs, t = q_ref.shape[0], q_ref.shape[1]
    wc = kw_ref.shape[1]
    nt = (((1,), (1,)), ((), ()))

    def scores(si):
        q = q_ref[si].astype(F32) * (HD ** -0.5)
        kwb = kw_ref[si].astype(BF16)
        knb = kn_ref[si].astype(BF16)
        out = []
        for kv in range(ATT_KV):
            qs = jnp.concatenate(
                [q[:, (kv * ATT_G + g) * HD:(kv * ATT_G + g + 1) * HD] for g in range(ATT_G)], axis=0
            ).astype(BF16)
            sl = slice(kv * HD, (kv + 1) * HD)
            out.append((lax.dot_general(qs, kwb[:, sl], nt, preferred_element_type=F32) + bw_ref[kv],
                        lax.dot_general(qs, knb[:, sl], nt, preferred_element_type=F32) + bn_ref[kv]))
        return out

    def finish(si, sts):
        kn = kn_ref[si]
        vn = vn_ref[si]
        kw = kw_ref[si]
        vw = vw_ref[si]
        ko_ref[si, 0:wc - t, :] = kw[t:, :]
        ko_ref[si, wc - t:wc, :] = kn.astype(F32)
        vo_ref[si, 0:wc - t, :] = vw[t:, :]
        vo_ref[si, wc - t:wc, :] = vn.astype(F32)
        vnb, vwb = vn.astype(BF16), vw.astype(BF16)
        probs = []
        for kv, (sw, sn) in enumerate(sts):
            sink = sink_ref[kv]
            m = jnp.maximum(jnp.maximum(jnp.max(sw, axis=-1, keepdims=True),
                                        jnp.max(sn, axis=-1, keepdims=True)), sink)
            pw = jnp.exp(sw - m)
            pn = jnp.exp(sn - m)
            den = (jnp.sum(pw, axis=-1, keepdims=True) + jnp.sum(pn, axis=-1, keepdims=True)
                   + jnp.exp(sink - m))
            probs.append((pw.astype(BF16), pn.astype(BF16), den))
        outs = []
        for kv, (pw, pn, den) in enumerate(probs):
            sl = slice(kv * HD, (kv + 1) * HD)
            o = (jnp.dot(pw, vwb[:, sl], preferred_element_type=F32)
                 + jnp.dot(pn, vnb[:, sl], preferred_element_type=F32)) / den
            for g in range(ATT_G):
                outs.append(o[g * t:(g + 1) * t])
        o_ref[si] = jnp.concatenate(outs, axis=-1).astype(o_ref.dtype)

    nxt = scores(0)
    for si in range(bs):
        cur = nxt
        if si + 1 < bs:
            nxt = scores(si + 1)
        finish(si, cur)


def _attn_sample(p3, kwin, vwin, bias_w, bias_n, sinkcol, bs):
    b, t, _ = p3.shape
    wc = kwin.shape[1]
    kcol, vcol = COL_AK // ATT_KV_W, COL_AV // ATT_KV_W
    seq3 = lambda i: (i, 0, 0)
    const3 = lambda i: (0, 0, 0)
    return pl.pallas_call(
        _attn_sample_kernel,
        grid=(b // bs,),
        in_specs=[pl.BlockSpec((bs, t, ATT_Q_W), seq3),
                  pl.BlockSpec((bs, t, ATT_KV_W), lambda i: (i, 0, kcol)),
                  pl.BlockSpec((bs, t, ATT_KV_W), lambda i: (i, 0, vcol)),
                  pl.BlockSpec((bs, wc, ATT_KV_W), seq3),
                  pl.BlockSpec((bs, wc, ATT_KV_W), seq3),
                  pl.BlockSpec(bias_w.shape, const3),
                  pl.BlockSpec(bias_n.shape, const3),
                  pl.BlockSpec(sinkcol.shape, const3)],
        out_specs=[pl.BlockSpec((bs, t, ATT_Q_W), seq3),
                   pl.BlockSpec((bs, wc, ATT_KV_W), seq3),
                   pl.BlockSpec((bs, wc, ATT_KV_W), seq3)],
        out_shape=[jax.ShapeDtypeStruct((b, t, ATT_Q_W), F32),
                   jax.ShapeDtypeStruct((b, wc, ATT_KV_W), F32),
                   jax.ShapeDtypeStruct((b, wc, ATT_KV_W), F32)],
        compiler_params=_cparams(("parallel",)),
        name="attn_sample",
    )(p3, p3, p3, kwin, vwin, bias_w, bias_n, sinkcol)


def _mlstm_kernel(bi_ref, bf_ref, qpre_ref, kpre_ref, v_ref, mo_ref, gc_ref, gr_ref,
                  cq_ref, ck_ref, cwq_ref, cwk_ref, cbq_ref, cbk_ref, g_ref,
                  c0_ref, n0_ref, m0_ref,
                  o_ref, c_ref, n_ref, m_ref, ext_scr):
    h = pl.program_id(1)
    c = pl.program_id(2)
    bs, L = qpre_ref.shape[0], qpre_ref.shape[1]

    @pl.when(c == 0)
    def _():
        c_ref[...] = c0_ref[...]
        n_ref[...] = n0_ref[...]
        m_ref[...] = m0_ref[...]
        ext_scr[:, 0, 0:8, :] = cq_ref[...]
        ext_scr[:, 1, 0:8, :] = ck_ref[...]

    b_i = bi_ref[h]
    b_f = bf_ref[h]
    ti = lax.broadcasted_iota(jnp.int32, (L, L), 0)
    si = lax.broadcasted_iota(jnp.int32, (L, L), 1)
    tri = si <= ti

    for bi in range(bs):
        ext_scr[bi, 0, 8:8 + L, :] = qpre_ref[bi].astype(F32)
        ext_scr[bi, 1, 8:8 + L, :] = kpre_ref[bi].astype(F32)

        def conv(idx, w_ref, b_ref):
            y = b_ref[...]
            for j in range(ML_CONV):
                lo = 8 - (ML_CONV - 1) + j
                y = y + ext_scr[bi, idx, lo:lo + L, :] * w_ref[j:j + 1, :]
            return y

        qc = conv(0, cwq_ref, cbq_ref)
        kc = conv(1, cwk_ref, cbk_ref)
        tail_q = ext_scr[bi, 0, L:L + 8, :]
        tail_k = ext_scr[bi, 1, L:L + 8, :]
        ext_scr[bi, 0, 0:8, :] = tail_q
        ext_scr[bi, 1, 0:8, :] = tail_k

        q = _silu(qc)
        k = _silu(kc) * (ML_DK ** -0.5)
        qb = q.astype(BF16)
        vb = v_ref[bi].astype(BF16)

        gc = gc_ref[bi, 0]
        gr = gr_ref[bi, 0]
        i_col = gc[:, 0:1] + b_i
        f_col = _log_sigmoid(gc[:, 1:2] + b_f)
        i_row = gr[0:1, :] + b_i
        f_row = _log_sigmoid(gr[1:2, :] + b_f)

        bcum_col = jnp.sum(jnp.where(tri, f_row, 0.0), axis=1, keepdims=True)
        bcum_row = jnp.sum(jnp.where(ti <= si, f_col, 0.0), axis=0, keepdims=True)
        b_last = jnp.sum(f_row, axis=1, keepdims=True)
        m_prev = m_ref[bi, 0]
        cmat = c_ref[bi, 0]
        nrow = n_ref[bi, 0]

        dmat = jnp.where(tri, bcum_col - bcum_row + i_row, NEG)
        inter = bcum_col + m_prev
        mt = jnp.maximum(inter, jnp.max(dmat, axis=1, keepdims=True))
        s = lax.dot_general(qb, k.astype(BF16), (((1,), (1,)), ((), ())),
                            preferred_element_type=F32) * jnp.exp(dmat - mt)
        w_in = jnp.exp(inter - mt)
        num = (jnp.dot(s.astype(BF16), vb, preferred_element_type=F32)
               + w_in * jnp.dot(qb, cmat.astype(BF16), preferred_element_type=F32))
        den = jnp.sum(s, axis=1, keepdims=True) + w_in * jnp.sum(q * nrow, axis=1, keepdims=True)
        hh = num / jnp.maximum(jnp.abs(den), jnp.exp(-mt))

        dec_col = b_last - bcum_col + i_col
        dec_row = b_last - bcum_row + i_row
        m_new = jnp.maximum(b_last + m_prev, jnp.max(dec_row, axis=1, keepdims=True))
        wk_col = jnp.exp(dec_col - m_new)
        decay = jnp.exp(b_last + m_prev - m_new)
        kw = wk_col * k
        c_ref[bi, 0] = decay * cmat + lax.dot_general(
            kw.astype(BF16), vb, (((0,), (0,)), ((), ())), preferred_element_type=F32)
        n_ref[bi, 0] = decay * nrow + jnp.sum(kw, axis=0, keepdims=True)
        m_ref[bi, 0] = m_new

        mu = jnp.mean(hh, axis=1, keepdims=True)
        xc = hh - mu
        var = jnp.mean(xc * xc, axis=1, keepdims=True)
        hn = xc * lax.rsqrt(var + LN_EPS) * g_ref[...]
        o_ref[bi] = (_sigmoid(mo_ref[bi].astype(F32)) * hn).astype(o_ref.dtype)


def _mlstm(p3, gates_col, gates_row, conv8, conv_w, conv_b, norm_g, b_i, b_f, c0, n0, m0, L, bs):
    b, t, _ = p3.shape
    nc = t // L
    assert bs == 1 or nc == 1
    smem = pl.BlockSpec(memory_space=pltpu.SMEM)
    qb0 = COL_MQK // ML_DK
    kb0 = (COL_MQK + ML_QK_W) // ML_DK
    vb0 = COL_MV // ML_DV
    ob0 = COL_MO // ML_DV
    kq0 = ML_QK_W // ML_DK
    return pl.pallas_call(
        _mlstm_kernel,
        grid=(b // bs, ML_H, nc),
        in_specs=[smem, smem,
                  pl.BlockSpec((bs, L, ML_DK), lambda i, h, c: (i, c, qb0 + h)),
                  pl.BlockSpec((bs, L, ML_DK), lambda i, h, c: (i, c, kb0 + h)),
                  pl.BlockSpec((bs, L, ML_DV), lambda i, h, c: (i, c, vb0 + h)),
                  pl.BlockSpec((bs, L, ML_DV), lambda i, h, c: (i, c, ob0 + h)),
                  pl.BlockSpec((bs, 1, L, 2), lambda i, h, c: (i * nc + c, h, 0, 0)),
                  pl.BlockSpec((bs, 1, 2, L), lambda i, h, c: (i * nc + c, h, 0, 0)),
                  pl.BlockSpec((bs, 8, ML_DK), lambda i, h, c: (i, 0, h)),
                  pl.BlockSpec((bs, 8, ML_DK), lambda i, h, c: (i, 0, kq0 + h)),
                  pl.BlockSpec((ML_CONV, ML_DK), lambda i, h, c: (0, h)),
                  pl.BlockSpec((ML_CONV, ML_DK), lambda i, h, c: (0, kq0 + h)),
                  pl.BlockSpec((1, ML_DK), lambda i, h, c: (0, h)),
                  pl.BlockSpec((1, ML_DK), lambda i, h, c: (0, kq0 + h)),
                  pl.BlockSpec((1, ML_DV), lambda i, h, c: (0, h)),
                  pl.BlockSpec((bs, 1, ML_DK, ML_DV), lambda i, h, c: (i, h, 0, 0)),
                  pl.BlockSpec((bs, 1, 1, ML_DK), lambda i, h, c: (i, h, 0, 0)),
                  pl.BlockSpec((bs, 1, 1, 1), lambda i, h, c: (i, h, 0, 0))],
        out_specs=[pl.BlockSpec((bs, L, ML_DV), lambda i, h, c: (i, c, h)),
                   pl.BlockSpec((bs, 1, ML_DK, ML_DV), lambda i, h, c: (i, h, 0, 0)),
                   pl.BlockSpec((bs, 1, 1, ML_DK), lambda i, h, c: (i, h, 0, 0)),
                   pl.BlockSpec((bs, 1, 1, 1), lambda i, h, c: (i, h, 0, 0))],
        out_shape=[jax.ShapeDtypeStruct((b, t, ML_V_W), F32),
                   jax.ShapeDtypeStruct((b, ML_H, ML_DK, ML_DV), F32),
                   jax.ShapeDtypeStruct((b, ML_H, 1, ML_DK), F32),
                   jax.ShapeDtypeStruct((b, ML_H, 1, 1), F32)],
        scratch_shapes=[pltpu.VMEM((bs, 2, L + 8, ML_DK), F32)],
        compiler_params=_cparams(("parallel", "parallel", "arbitrary")),
        name="mlstm",
    )(b_i, b_f, p3, p3, p3, p3, gates_col, gates_row, conv8, conv8, conv_w, conv_w,
      conv_b, conv_b, norm_g, c0, n0, m0)


def _layer_norm(r, g, b):
    mu = jnp.mean(r, axis=-1, keepdims=True)
    xc = r - mu
    var = jnp.mean(xc * xc, axis=-1, keepdims=True)
    return xc * lax.rsqrt(var + LN_EPS) * g + b


def _mix_kernel(att_ref, hm_ref, ga0_ref, ga1_ref, gb0_ref, gb1_ref, x_ref, g1_ref, sc2_ref, sh2_ref,
                wa_ref, wb_ref, wo_ref, lg_ref, lb_ref, x1_ref, h2t_ref):
    ya = jnp.dot(att_ref[...].astype(BF16), wa_ref[...], preferred_element_type=F32)
    yb = jnp.dot(hm_ref[...].astype(BF16), wb_ref[...], preferred_element_type=F32)
    ga = jnp.concatenate([ga0_ref[...], ga1_ref[...]], axis=1).astype(F32)
    gb = jnp.concatenate([gb0_ref[...], gb1_ref[...]], axis=1).astype(F32)
    z = _sigmoid(ga) * ya + _sigmoid(gb) * yb
    mix = jnp.dot(z.astype(BF16), wo_ref[...], preferred_element_type=F32)
    shp = x_ref.shape
    r = ALPHA * x_ref[...] + g1_ref[...] * mix.reshape(shp)
    x1 = _layer_norm(r, lg_ref[...], lb_ref[...])
    x1_ref[...] = x1
    h2 = x1 * (1.0 + sc2_ref[...]) + sh2_ref[...]
    h2t_ref[...] = h2.reshape(mix.shape).T.astype(BF16)


def _mix(att2, hm2, p2, x, g1, sc2, sh2, wa, wb, wo, ln_g, ln_b, bb, tt):
    bx, tx, d = x.shape
    tm = bb * tt
    ntok = bx * tx
    tpb = tx // tt
    xmap = lambda i: (i // tpb if bb == 1 else i, i % tpb if bb == 1 else 0, 0)
    mmap = lambda i: (i // tpb if bb == 1 else i, 0, 0)
    hw = d // 2
    wspec = lambda shp: pl.BlockSpec(shp, lambda i: (0, 0), pipeline_mode=pl.Buffered(1))
    return pl.pallas_call(
        _mix_kernel,
        grid=(ntok // tm,),
        in_specs=[pl.BlockSpec((tm, d), lambda i: (i, 0)),
                  pl.BlockSpec((tm, d), lambda i: (i, 0)),
                  pl.BlockSpec((tm, hw), lambda i: (i, COL_GA // hw)),
                  pl.BlockSpec((tm, hw), lambda i: (i, COL_GA // hw + 1)),
                  pl.BlockSpec((tm, hw), lambda i: (i, COL_GB // hw)),
                  pl.BlockSpec((tm, hw), lambda i: (i, COL_GB // hw + 1)),
                  pl.BlockSpec((bb, tt, d), xmap),
                  pl.BlockSpec((bb, 1, d), mmap),
                  pl.BlockSpec((bb, 1, d), mmap),
                  pl.BlockSpec((bb, 1, d), mmap),
                  wspec(wa.shape), wspec(wb.shape), wspec(wo.shape),
                  pl.BlockSpec((1, d), lambda i: (0, 0)),
                  pl.BlockSpec((1, d), lambda i: (0, 0))],
        out_specs=[pl.BlockSpec((bb, tt, d), xmap),
                   pl.BlockSpec((d, tm), lambda i: (0, i))],
        out_shape=[jax.ShapeDtypeStruct((bx, tx, d), F32),
                   jax.ShapeDtypeStruct((d, ntok), BF16)],
        compiler_params=_cparams(("parallel",)),
        name="mix",
    )(att2, hm2, p2, p2, p2, p2, x, g1, sc2, sh2, wa, wb, wo, ln_g, ln_b)


def _top16(val, kio):
    work = val
    rank = jnp.full(val.shape, float(TOPK), F32)
    tops = []
    nkeys = val.shape[0]
    for r in range(TOPK):
        m = jnp.max(work, axis=0, keepdims=True)
        idx = jnp.min(jnp.where(work == m, kio, float(nkeys)), axis=0, keepdims=True)
        hit = kio == idx
        rank = jnp.where(hit, float(r), rank)
        work = jnp.where(hit, -jnp.inf, work)
        tops.append(m)
    return rank, tops


def _top16_distinct(val, want_rank):
    work = val
    rank = jnp.full(val.shape, float(TOPK), F32) if want_rank else None
    tops = []
    for r in range(TOPK):
        m = jnp.max(work, axis=0, keepdims=True)
        hit = work == m
        if want_rank:
            rank = jnp.where(hit, float(r), rank)
        work = jnp.where(hit, -jnp.inf, work)
        tops.append(m)
    removed = jnp.sum(jnp.where(work == -jnp.inf, 1.0, 0.0), axis=0, keepdims=True)
    return rank, tops, removed


def _peer_select_kernel(h2t_ref, wq_ref, keys_ref, rb_ref, eb_ref, cnt_ref, ea_ref, q_scr):
    tl = h2t_ref.shape[1]
    q_scr[...] = jnp.dot(wq_ref[...], h2t_ref[...], preferred_element_type=F32).astype(BF16)
    kio = lax.broadcasted_iota(jnp.int32, (NK, tl), 0).astype(F32)
    rio = lax.broadcasted_iota(jnp.int32, (TOPK, tl), 0).astype(F32)

    def head(hd, exact):
        qa = q_scr[pl.ds(pl.multiple_of(hd * 2 * NK, NK), NK), :]
        qb = q_scr[pl.ds(pl.multiple_of(hd * 2 * NK + NK, NK), NK), :]
        a = jnp.dot(keys_ref[2 * hd], qa, preferred_element_type=F32)
        b = jnp.dot(keys_ref[2 * hd + 1], qb, preferred_element_type=F32)
        if exact:
            ra, atop = _top16(a, kio)
            rb, btop = _top16(b, kio)
            tied = None
        else:
            _, atop, na = _top16_distinct(a, False)
            rb, btop, nb = _top16_distinct(b, True)
            tied = jnp.max(jnp.maximum(na, nb)) > float(TOPK)
        asort = jnp.concatenate(atop, axis=0)
        cnt = jnp.zeros((TOPK, tl), F32)
        front = asort + btop[0]
        for _ in range(TOPK):
            m = jnp.max(front, axis=0, keepdims=True)
            idx = jnp.min(jnp.where(front == m, rio, float(TOPK)), axis=0, keepdims=True)
            hit = rio == idx
            cnt = cnt + jnp.where(hit, 1.0, 0.0)
            nxt = jnp.full((TOPK, tl), -jnp.inf, F32)
            for cc in range(1, TOPK):
                nxt = jnp.where(cnt == float(cc), btop[cc], nxt)
            front = jnp.where(hit, asort + nxt, front)
        ea_s = jnp.exp(asort - atop[0])
        pref = jnp.zeros((1, tl), F32)
        pbsel = jnp.zeros((TOPK, tl), F32)
        for cc in range(1, TOPK + 1):
            pref = pref + jnp.exp(btop[cc - 1] - btop[0])
            pbsel = jnp.where(cnt == float(cc), pref, pbsel)
        z = jnp.sum(ea_s * pbsel, axis=0, keepdims=True)
        cnti = jnp.zeros((NK, tl), F32)
        for r in range(TOPK):
            match = (ra == float(r)) if exact else (a == atop[r])
            cnti = jnp.where(match, cnt[r:r + 1, :], cnti)
        rb_ref[hd] = rb.astype(rb_ref.dtype)
        eb_ref[hd] = jnp.exp(b - btop[0]).astype(eb_ref.dtype)
        cnt_ref[hd] = cnti
        ea_ref[hd] = jnp.exp(a - atop[0]) * (0.5 / z)
        return tied

    def per_head(hd, carry):
        tied = head(hd, False)

        @pl.when(tied)
        def _():
            head(hd, True)

        return carry

    lax.fori_loop(0, PEER_H, per_head, 0)


def _peer_select(h2t, wqt, keys, tl):
    d, ntok = h2t.shape
    out = jax.ShapeDtypeStruct((PEER_H, NK, ntok), F32)
    outb = jax.ShapeDtypeStruct((PEER_H, NK, ntok), jnp.bfloat16)
    ospec = pl.BlockSpec((PEER_H, NK, tl), lambda i: (0, 0, i))
    return pl.pallas_call(
        _peer_select_kernel,
        grid=(ntok // tl,),
        in_specs=[pl.BlockSpec((d, tl), lambda i: (0, i)),
                  pl.BlockSpec(wqt.shape, lambda i: (0, 0), pipeline_mode=pl.Buffered(1)),
                  pl.BlockSpec(keys.shape, lambda i: (0, 0, 0))],
        out_specs=[ospec, ospec, ospec, ospec],
        out_shape=[outb, outb, out, out],
        scratch_shapes=[pltpu.VMEM((wqt.shape[0], tl), BF16)],
        compiler_params=_cparams(("parallel",)),
        name="peer_select",
    )(h2t, wqt, keys)


def _peer_dense_kernel(h2t_ref, u_ref, vt_ref, rb_ref, eb_ref, cnt_ref, ea_ref, o_ref,
                       act0_ref, act1_ref, mt_ref, *, n_e):
    s = pl.program_id(0)
    e_prev = jnp.maximum(s - 1, 0) % n_e
    half = u_ref.shape[0] // 2
    zero = jnp.zeros((), rb_ref.dtype)

    @pl.when(s == 0)
    def _():
        act1_ref[...] = jnp.zeros_like(act1_ref)

    @pl.when(e_prev == 0)
    def _():
        o_ref[...] = jnp.zeros_like(o_ref)

    tm = h2t_ref.shape[1]
    pk = rb_ref.shape[2]

    def row(ref, hd, il):
        return jnp.broadcast_to(ref[hd, il:il + 1, :], (pk, tm)).astype(rb_ref.dtype)[None]

    def gated(old_ref, hf):
        for il in range(hf * half // NK, (hf + 1) * half // NK):
            gate = None
            for hd in range(PEER_H):
                term = jnp.where(rb_ref[hd] < row(cnt_ref, hd, il), eb_ref[hd] * row(ea_ref, hd, il), zero)
                gate = term if gate is None else gate + term
            rows = slice(il * NK, (il + 1) * NK)
            a = old_ref[rows, :]
            half_gelu2 = a * (1.0 + lax.erf(a * (2.0 ** -0.5)))
            mt_ref[rows, :] = (gate.reshape(NK, tm).astype(F32) * half_gelu2).astype(BF16)

    def phases(new_ref, old_ref):
        new_ref[...] = jnp.dot(u_ref[...], h2t_ref[...], preferred_element_type=F32)
        for hf in range(2):
            rows = slice(hf * half, (hf + 1) * half)
            gated(old_ref, hf)
            o_ref[...] += jnp.dot(vt_ref[:, rows], mt_ref[rows, :], preferred_element_type=F32)

    @pl.when(s % 2 == 0)
    def _():
        phases(act0_ref, act1_ref)

    @pl.when(s % 2 == 1)
    def _():
        phases(act1_ref, act0_ref)


def _peer_dense(h2t, u, vt, rb, eb, cnti, ea, tm):
    d, ntok = h2t.shape
    et = 8 * NK
    n_e = u.shape[0] // et
    n_items = (ntok // tm) * n_e
    item_a = lambda s: jnp.minimum(s, n_items - 1)
    item_b = lambda s: jnp.maximum(s - 1, 0)
    pk = 16
    rb = rb.reshape(PEER_H, NK // pk, pk, ntok)
    eb = eb.reshape(PEER_H, NK // pk, pk, ntok)
    rspec = pl.BlockSpec((PEER_H, NK // pk, pk, tm), lambda s: (0, 0, 0, item_b(s) // n_e))
    cspec = pl.BlockSpec((PEER_H, 8, tm), lambda s: (0, item_b(s) % n_e, item_b(s) // n_e))
    return pl.pallas_call(
        functools.partial(_peer_dense_kernel, n_e=n_e),
        grid=(n_items + 1,),
        in_specs=[pl.BlockSpec((d, tm), lambda s: (0, item_a(s) // n_e)),
                  pl.BlockSpec((et, d), lambda s: (item_a(s) % n_e, 0)),
                  pl.BlockSpec((d, et), lambda s: (0, item_b(s) % n_e)),
                  rspec, rspec, cspec, cspec],
        out_specs=pl.BlockSpec((d, tm), lambda s: (0, item_b(s) // n_e)),
        out_shape=jax.ShapeDtypeStruct((d, ntok), F32),
        scratch_shapes=[pltpu.VMEM((et, tm), F32), pltpu.VMEM((et, tm), F32),
                        pltpu.VMEM((et, tm), BF16)],
        compiler_params=_cparams(("arbitrary",)),
        name="peer_dense",
    )(h2t, u, vt, rb, eb, cnti, ea)


def _final_kernel(x1_ref, po_ref, g2_ref, lg_ref, lb_ref, y_ref):
    shp = x1_ref.shape
    r = ALPHA * x1_ref[...] + g2_ref[...] * po_ref[...].T.reshape(shp)
    y_ref[...] = _layer_norm(r, lg_ref[...], lb_ref[...])


def _final(x1, po, g2, ln_g, ln_b, bb, tt):
    bx, tx, d = x1.shape
    tm = bb * tt
    tpb = tx // tt
    xmap = lambda i: (i // tpb if bb == 1 else i, i % tpb if bb == 1 else 0, 0)
    mmap = lambda i: (i // tpb if bb == 1 else i, 0, 0)
    return pl.pallas_call(
        _final_kernel,
        grid=(bx * tx // tm,),
        in_specs=[pl.BlockSpec((bb, tt, d), xmap),
                  pl.BlockSpec((d, tm), lambda i: (0, i)),
                  pl.BlockSpec((bb, 1, d), mmap),
                  pl.BlockSpec((1, d), lambda i: (0, 0)),
                  pl.BlockSpec((1, d), lambda i: (0, 0))],
        out_specs=pl.BlockSpec((bb, tt, d), xmap),
        out_shape=jax.ShapeDtypeStruct((bx, tx, d), F32),
        compiler_params=_cparams(("parallel",)),
        name="final_ln",
    )(x1, po, g2, ln_g, ln_b)


def _tile(bx, tx, want):
    if tx >= want:
        return 1, want
    return want // tx, tx


def _layer(x, ada, state, prm):
    bx, tx, d = x.shape
    ntok = bx * tx
    sh1, sc1, g1, sh2, sc2, g2 = [a[:, None, :] for a in jnp.split(ada, 6, axis=-1)]

    bb, tt = _tile(bx, tx, min(1024, ntok))
    p2, gates2 = _inproj(x, sc1, sh1, prm["w_in1"], prm["w_in2"], prm["w_if"], bb, tt)
    p3 = p2.reshape(bx, tx, IN_MAIN)

    if state is None:
        att = _attn_prompt(p3, prm["bias_prompt"], prm["sink_prompt"])
        keep = min(WINDOW, tx)
        k_keep = p3[:, tx - keep:, COL_AK:COL_AK + ATT_KV_W].astype(F32)
        v_keep = p3[:, tx - keep:, COL_AV:COL_AV + ATT_KV_W].astype(F32)
        conv_buf = jnp.zeros((bx, ML_CONV - 1, 2 * ML_QK_W), F32)
        c0 = jnp.zeros((bx, ML_H, ML_DK, ML_DV), F32)
        n0 = jnp.zeros((bx, ML_H, ML_DK), F32)
        m0 = jnp.zeros((bx, ML_H), F32)
    else:
        kwin, vwin, conv_buf, c0, n0, m0 = state
        wc = kwin.shape[1]
        att, k_keep, v_keep = _attn_sample(
            p3, kwin.reshape(bx, wc, ATT_KV_W), vwin.reshape(bx, wc, ATT_KV_W),
            prm["bias_sample_w"], prm["bias_sample_n"], prm["sink_sample"], math.gcd(bx, 8))
    k_keep = k_keep.reshape(bx, -1, ATT_KV, HD)
    v_keep = v_keep.reshape(bx, -1, ATT_KV, HD)

    L = min(ML_CHUNK_MAX, tx)
    nc = tx // L
    gates = gates2[:, :2 * ML_H].reshape(bx * nc, L, 2, ML_H)
    gates_col = jnp.transpose(gates, (0, 3, 1, 2))
    gates_row = jnp.transpose(gates, (0, 3, 2, 1))
    conv8 = jnp.pad(conv_buf.astype(F32), ((0, 0), (8 - (ML_CONV - 1), 0), (0, 0)))
    hm, c1, n1, m1 = _mlstm(p3, gates_col, gates_row, conv8, prm["conv_w"], prm["conv_b"],
                            prm["norm_g"], prm["b_i"], prm["b_f"], c0.astype(F32),
                            n0.astype(F32).reshape(bx, ML_H, 1, ML_DK),
                            m0.astype(F32).reshape(bx, ML_H, 1, 1), L,
                            math.gcd(bx, 8) if nc == 1 else 1)
    conv_keep = p3[:, tx - (ML_CONV - 1):, COL_MQK:COL_MQK + 2 * ML_QK_W].astype(F32)
    new_state = (k_keep, v_keep, conv_keep, c1, n1.reshape(bx, ML_H, ML_DK), m1.reshape(bx, ML_H))

    bb, tt = _tile(bx, tx, min(256, ntok))
    x1, h2t = _mix(att.reshape(ntok, ATT_Q_W), hm.reshape(ntok, ML_V_W), p2, x, g1, sc2, sh2,
                   prm["w_a"], prm["w_b"], prm["w_out"], prm["ln1_g"], prm["ln1_b"], bb, tt)

    rb, eb, cnti, ea = _peer_select(h2t, prm["w_pqt"], prm["keys"], min(256, ntok))
    po = _peer_dense(h2t, prm["peer_u"], prm["peer_vt"], rb, eb, cnti, ea, min(512, ntok))

    bb, tt = _tile(bx, tx, min(512, ntok))
    y = _final(x1, po, g2, prm["ln2_g"], prm["ln2_b"], bb, tt)
    return y, new_state


def kernel(x_prompt, x_sample, c_prompt, c_sample, cache_k_win, cache_v_win, state_conv, state_C, state_n, state_m, rel_bias_table, w_ada, b_ada, w_in, b_i, b_f, conv_w, conv_b, ml_norm_g, att_sinks, w_a, w_b, w_out, ln1_g, ln1_b, w_pq, peer_sub_keys, peer_u, peer_v, ln2_g, ln2_b):
    assert w_ada.shape[0] == DEPTH
    d = x_prompt.shape[-1]
    bp = x_prompt.shape[0]
    ts = x_sample.shape[1]
    wc = cache_k_win.shape[2]

    wi = w_in[0]
    n_if = 2 * ML_H
    assert wi.shape[1] == IN_MAIN + n_if
    w_in1 = wi[:, :IN_W1].astype(BF16)
    w_in2 = wi[:, IN_W1 + n_if:].astype(BF16)
    w_if = jnp.pad(wi[:, IN_W1:IN_W1 + n_if], ((0, 0), (0, IF_PAD - n_if))).astype(BF16)

    sinks = att_sinks[0].astype(F32).reshape(ATT_KV, ATT_G, 1, 1)
    blk = WINDOW
    dist_p = (jnp.arange(blk)[:, None] + blk) - jnp.arange(2 * blk)[None, :]
    dist_s = (jnp.arange(ts)[:, None] + wc) - jnp.arange(wc + ts)[None, :]
    bias_s = _bias_table(rel_bias_table, dist_s)
    bias_p = _bias_table_t(rel_bias_table, dist_p)
    prm = {
        "w_in1": w_in1, "w_in2": w_in2, "w_if": w_if,
        "bias_prompt": jnp.stack([jnp.where(jnp.arange(2 * blk)[None, :, None] < blk, NEG, bias_p), bias_p]),
        "sink_prompt": jnp.broadcast_to(att_sinks[0].astype(F32)[:, None, None], (ATT_HEADS, 1, blk)),
        "bias_sample_w": bias_s[:, :, :wc],
        "bias_sample_n": bias_s[:, :, wc:],
        "sink_sample": jnp.broadcast_to(sinks, (ATT_KV, ATT_G, ts, 1)).reshape(ATT_KV, ATT_G * ts, 1),
        "conv_w": conv_w[0], "conv_b": conv_b[0].reshape(1, -1), "norm_g": ml_norm_g[0].reshape(1, -1),
        "b_i": b_i[0], "b_f": b_f[0],
        "w_a": w_a[0].astype(BF16), "w_b": w_b[0].astype(BF16), "w_out": w_out[0].astype(BF16),
        "ln1_g": ln1_g[0].reshape(1, -1), "ln1_b": ln1_b[0].reshape(1, -1),
        "w_pqt": w_pq[0].T.astype(BF16),
        "keys": peer_sub_keys[0].reshape(2 * PEER_H, NK, -1).astype(BF16),
        "peer_u": peer_u[0].astype(BF16),
        "peer_vt": peer_v[0].T.astype(BF16),
        "ln2_g": ln2_g[0].reshape(1, -1), "ln2_b": ln2_b[0].reshape(1, -1),
    }

    ada = _ada(jnp.concatenate([c_prompt, c_sample], axis=0), w_ada[0], b_ada[0])
    yp, sp = _layer(x_prompt, ada[:bp], None, prm)
    ys, ss = _layer(x_sample, ada[bp:],
                    (cache_k_win[0], cache_v_win[0], state_conv[0], state_C[0], state_n[0], state_m[0]), prm)
    stack = lambda s: [a[None] for a in s]
    return (yp, ys, *stack(sp), *stack(ss))
```

```python
import functools
import math

import jax
import jax.numpy as jnp
from jax import lax
from jax.experimental import pallas as pl
from jax.experimental.pallas import tpu as pltpu

F32 = jnp.float32
BF16 = jnp.bfloat16

ATT_HEADS = 32
ATT_KV = 8
ATT_G = ATT_HEADS // ATT_KV
HD = 64
WINDOW = 128
N_BUCKETS = 32
MAX_DISTANCE = 128
ML_H = 4
ML_DK = 256
ML_DV = 512
ML_CONV = 4
PEER_H = 8
NK = 128
TOPK = 16
ATT_Q_W = ATT_HEADS * HD
ATT_KV_W = ATT_KV * HD
ML_QK_W = ML_H * ML_DK
ML_V_W = ML_H * ML_DV
DEPTH = 1
ALPHA = (2.0 * DEPTH) ** 0.25
LN_EPS = 1e-5
NEG = -1e30

COL_AQ = 0
COL_AK = 2048
COL_AV = 2560
COL_MQK = 3072
COL_MV = 5120
IN_W1 = 7168
COL_MO = 7168
COL_GA = 9216
COL_GB = 11264
IN_MAIN = 13312
IN_TN = 1024
IF_PAD = 128

ML_CHUNK_MAX = 256
VMEM_LIMIT = 56 * 1024 * 1024


def _cparams(sem):
    return pltpu.CompilerParams(dimension_semantics=sem, vmem_limit_bytes=VMEM_LIMIT)


def _sigmoid(x):
    return 1.0 / (1.0 + jnp.exp(-x))


def _silu(x):
    return x * _sigmoid(x)


def _log_sigmoid(x):
    return jnp.minimum(x, 0.0) - jnp.log1p(jnp.exp(-jnp.abs(x)))


def _ada_kernel(c_ref, w_ref, b_ref, o_ref):
    c = c_ref[...]
    a = _silu(c).astype(BF16)
    o_ref[...] = jnp.dot(a, w_ref[...].astype(BF16), preferred_element_type=F32) + b_ref[...]


def _ada(c_all, w_ada, b_ada):
    r, d = c_all.shape
    n = w_ada.shape[1]
    tn = 1024
    return pl.pallas_call(
        _ada_kernel,
        grid=(n // tn,),
        in_specs=[pl.BlockSpec((r, d), lambda j: (0, 0)),
                  pl.BlockSpec((d, tn), lambda j: (0, j)),
                  pl.BlockSpec((1, tn), lambda j: (0, j))],
        out_specs=pl.BlockSpec((r, tn), lambda j: (0, j)),
        out_shape=jax.ShapeDtypeStruct((r, n), F32),
        compiler_params=_cparams(("parallel",)),
        name="ada",
    )(c_all, w_ada, b_ada.reshape(1, n))


def _inproj_kernel(x_ref, sc_ref, sh_ref, w1_ref, w2_ref, wif_ref, p_ref, if_ref, h_scr, *, n1):
    j = pl.program_id(1)

    @pl.when(j == 0)
    def _():
        h = x_ref[...] * (1.0 + sc_ref[...]) + sh_ref[...]
        hb = h.reshape(h_scr.shape).astype(BF16)
        h_scr[...] = hb
        if_ref[...] = jnp.dot(hb, wif_ref[...], preferred_element_type=F32)

    @pl.when(j < n1)
    def _():
        p_ref[...] = jnp.dot(h_scr[...], w1_ref[...], preferred_element_type=F32).astype(p_ref.dtype)

    @pl.when(j >= n1)
    def _():
        p_ref[...] = jnp.dot(h_scr[...], w2_ref[...], preferred_element_type=F32).astype(p_ref.dtype)


def _inproj(x, sc, sh, w1, w2, wif, bb, tt):
    bx, tx, d = x.shape
    tm = bb * tt
    ntok = bx * tx
    n1 = w1.shape[1] // IN_TN
    n2 = w2.shape[1] // IN_TN
    tpb = tx // tt
    xmap = lambda i, j: (i // tpb if bb == 1 else i, i % tpb if bb == 1 else 0, 0)
    mmap = lambda i, j: (i // tpb if bb == 1 else i, 0, 0)
    return pl.pallas_call(
        functools.partial(_inproj_kernel, n1=n1),
        grid=(ntok // tm, n1 + n2),
        in_specs=[pl.BlockSpec((bb, tt, d), xmap),
                  pl.BlockSpec((bb, 1, d), mmap),
                  pl.BlockSpec((bb, 1, d), mmap),
                  pl.BlockSpec((d, IN_TN), lambda i, j: (0, jnp.minimum(j, n1 - 1))),
                  pl.BlockSpec((d, IN_TN), lambda i, j: (0, jnp.maximum(j - n1, 0))),
                  pl.BlockSpec((d, IF_PAD), lambda i, j: (0, 0))],
        out_specs=[pl.BlockSpec((tm, IN_TN), lambda i, j: (i, j)),
                   pl.BlockSpec((tm, IF_PAD), lambda i, j: (i, 0))],
        out_shape=[jax.ShapeDtypeStruct((ntok, IN_MAIN), BF16),
                   jax.ShapeDtypeStruct((ntok, IF_PAD), F32)],
        scratch_shapes=[pltpu.VMEM((tm, d), BF16)],
        compiler_params=_cparams(("parallel", "arbitrary")),
        name="inproj",
    )(x, sc, sh, w1, w2, wif)


def _t5_bucket(dist):
    max_exact = N_BUCKETS // 2
    d = jnp.maximum(dist, 0)
    dl = jnp.maximum(d, max_exact).astype(F32)
    large = max_exact + (jnp.log(dl / max_exact) / math.log(MAX_DISTANCE / max_exact)
                         * (N_BUCKETS - max_exact)).astype(jnp.int32)
    large = jnp.minimum(large, N_BUCKETS - 1)
    return jnp.where(d < max_exact, d, large)


def _bias_table(rel_table, dist):
    lq, lk = dist.shape
    band = (dist >= 0) & (dist < WINDOW)
    bias = jnp.transpose(rel_table[_t5_bucket(dist)], (2, 0, 1)).astype(F32)
    bias = jnp.where(band[None], bias, NEG)
    return bias.reshape(ATT_KV, ATT_G * lq, lk)


def _bias_table_t(rel_table, dist):
    lq, lk = dist.shape
    bucket = _t5_bucket(dist).T.reshape(1, lk * lq)
    onehot = (bucket == jnp.arange(N_BUCKETS)[:, None]).astype(F32)
    bias = jnp.dot(rel_table.T.astype(F32), onehot, precision=lax.Precision.HIGHEST)
    band = ((dist >= 0) & (dist < WINDOW)).T.reshape(1, lk * lq)
    return jnp.where(band, bias, NEG).reshape(-1, lk, lq)


def _attn_prompt_kernel(q_ref, kp_ref, kc_ref, vp_ref, vc_ref, bias_ref, sink_ref, o_ref):
    qt = (q_ref[0] * (HD ** -0.5)).T.astype(BF16)
    kcat = jnp.concatenate([kp_ref[0], kc_ref[0]], axis=0).astype(BF16)
    vt = jnp.concatenate([vp_ref[0], vc_ref[0]], axis=0).T.astype(BF16)

    def scores(kv):
        kh = kcat[:, kv * HD:(kv + 1) * HD]
        return [jnp.dot(kh, qt[h * HD:(h + 1) * HD, :], preferred_element_type=F32) + bias_ref[0, h]
                for h in range(kv * ATT_G, (kv + 1) * ATT_G)]

    def finish(kv, sts):
        vth = vt[kv * HD:(kv + 1) * HD, :]
        ps, dens = [], []
        for g, st in enumerate(sts):
            sink = sink_ref[kv * ATT_G + g]
            m = jnp.maximum(jnp.max(st, axis=0, keepdims=True), sink)
            p = jnp.exp(st - m)
            dens.append(jnp.sum(p, axis=0, keepdims=True) + jnp.exp(sink - m))
            ps.append(p.astype(BF16))
        outs = [jnp.dot(vth, p, preferred_element_type=F32) / den for p, den in zip(ps, dens)]
        for g in range(0, ATT_G, 2):
            h = kv * ATT_G + g
            o_ref[0, :, h * HD:(h + 2) * HD] = jnp.concatenate(outs[g:g + 2], axis=0).T.astype(o_ref.dtype)

    nxt = scores(0)
    for kv in range(ATT_KV):
        cur = nxt
        if kv + 1 < ATT_KV:
            nxt = scores(kv + 1)
        finish(kv, cur)


def _attn_prompt(p3, bias2, sinkrow):
    b, t, _ = p3.shape
    blk = WINDOW
    nb = t // blk
    kcol, vcol = COL_AK // ATT_KV_W, COL_AV // ATT_KV_W
    return pl.pallas_call(
        _attn_prompt_kernel,
        grid=(b, nb),
        in_specs=[pl.BlockSpec((1, blk, ATT_Q_W), lambda i, n: (i, n, 0)),
                  pl.BlockSpec((1, blk, ATT_KV_W), lambda i, n: (i, jnp.maximum(n - 1, 0), kcol)),
                  pl.BlockSpec((1, blk, ATT_KV_W), lambda i, n: (i, n, kcol)),
                  pl.BlockSpec((1, blk, ATT_KV_W), lambda i, n: (i, jnp.maximum(n - 1, 0), vcol)),
                  pl.BlockSpec((1, blk, ATT_KV_W), lambda i, n: (i, n, vcol)),
                  pl.BlockSpec((1,) + bias2.shape[1:], lambda i, n: (jnp.minimum(n, 1), 0, 0, 0)),
                  pl.BlockSpec(sinkrow.shape, lambda i, n: (0, 0, 0))],
        out_specs=pl.BlockSpec((1, blk, ATT_Q_W), lambda i, n: (i, n, 0)),
        out_shape=jax.ShapeDtypeStruct((b, t, ATT_Q_W), BF16),
        compiler_params=_cparams(("parallel", "arbitrary")),
        name="attn_prompt",
    )(p3, p3, p3, p3, p3, bias2, sinkrow)


def _attn_sample_kernel(q_ref, kn_ref, vn_ref, kw_ref, vw_ref, bw_ref, bn_ref, sink_ref,
                        o_ref, ko_ref, vo_ref):
    bs, t = q_ref.shape[0], q_ref.shape[1]
    wc = kw_ref.shape[1]
    nt = (((1,), (1,)), ((), ()))

    def scores(si):
        q = q_ref[si].astype(F32) * (HD ** -0.5)
        kwb = kw_ref[si].astype(BF16)
        knb = kn_ref[si].astype(BF16)
        out = []
        for kv in range(ATT_KV):
            qs = jnp.concatenate(
                [q[:, (kv * ATT_G + g) * HD:(kv * ATT_G + g + 1) * HD] for g in range(ATT_G)], axis=0
            ).astype(BF16)
            sl = slice(kv * HD, (kv + 1) * HD)
            out.append((lax.dot_general(qs, kwb[:, sl], nt, preferred_element_type=F32) + bw_ref[kv],
                        lax.dot_general(qs, knb[:, sl], nt, preferred_element_type=F32) + bn_ref[kv]))
        return out

    def finish(si, sts):
        kn = kn_ref[si]
        vn = vn_ref[si]
        kw = kw_ref[si]
        vw = vw_ref[si]
        ko_ref[si, 0:wc - t, :] = kw[t:, :]
        ko_ref[si, wc - t:wc, :] = kn.astype(F32)
        vo_ref[si, 0:wc - t, :] = vw[t:, :]
        vo_ref[si, wc - t:wc, :] = vn.astype(F32)
        vnb, vwb = vn.astype(BF16), vw.astype(BF16)
        probs = []
        for kv, (sw, sn) in enumerate(sts):
            sink = sink_ref[kv]
            m = jnp.maximum(jnp.maximum(jnp.max(sw, axis=-1, keepdims=True),
                                        jnp.max(sn, axis=-1, keepdims=True)), sink)
            pw = jnp.exp(sw - m)
            pn = jnp.exp(sn - m)
            den = (jnp.sum(pw, axis=-1, keepdims=True) + jnp.sum(pn, axis=-1, keepdims=True)
                   + jnp.exp(sink - m))
            probs.append((pw.astype(BF16), pn.astype(BF16), den))
        outs = []
        for kv, (pw, pn, den) in enumerate(probs):
            sl = slice(kv * HD, (kv + 1) * HD)
            o = (jnp.dot(pw, vwb[:, sl], preferred_element_type=F32)
                 + jnp.dot(pn, vnb[:, sl], preferred_element_type=F32)) / den
            for g in range(ATT_G):
                outs.append(o[g * t:(g + 1) * t])
        o_ref[si] = jnp.concatenate(outs, axis=-1).astype(o_ref.dtype)

    nxt = scores(0)
    for si in range(bs):
        cur = nxt
        if si + 1 < bs:
            nxt = scores(si + 1)
        finish(si, cur)


def _attn_sample(p3, kwin, vwin, bias_w, bias_n, sinkcol, bs):
    b, t, _ = p3.shape
    wc = kwin.shape[1]
    kcol, vcol = COL_AK // ATT_KV_W, COL_AV // ATT_KV_W
    seq3 = lambda i: (i, 0, 0)
    const3 = lambda i: (0, 0, 0)
    return pl.pallas_call(
        _attn_sample_kernel,
        grid=(b // bs,),
        in_specs=[pl.BlockSpec((bs, t, ATT_Q_W), seq3),
                  pl.BlockSpec((bs, t, ATT_KV_W), lambda i: (i, 0, kcol)),
                  pl.BlockSpec((bs, t, ATT_KV_W), lambda i: (i, 0, vcol)),
                  pl.BlockSpec((bs, wc, ATT_KV_W), seq3),
                  pl.BlockSpec((bs, wc, ATT_KV_W), seq3),
                  pl.BlockSpec(bias_w.shape, const3),
                  pl.BlockSpec(bias_n.shape, const3),
                  pl.BlockSpec(sinkcol.shape, const3)],
        out_specs=[pl.BlockSpec((bs, t, ATT_Q_W), seq3),
                   pl.BlockSpec((bs, wc, ATT_KV_W), seq3),
                   pl.BlockSpec((bs, wc, ATT_KV_W), seq3)],
        out_shape=[jax.ShapeDtypeStruct((b, t, ATT_Q_W), F32),
                   jax.ShapeDtypeStruct((b, wc, ATT_KV_W), F32),
                   jax.ShapeDtypeStruct((b, wc, ATT_KV_W), F32)],
        compiler_params=_cparams(("parallel",)),
        name="attn_sample",
    )(p3, p3, p3, kwin, vwin, bias_w, bias_n, sinkcol)


def _mlstm_kernel(bi_ref, bf_ref, qpre_ref, kpre_ref, v_ref, mo_ref, gc_ref, gr_ref,
                  cq_ref, ck_ref, cwq_ref, cwk_ref, cbq_ref, cbk_ref, g_ref,
                  c0_ref, n0_ref, m0_ref,
                  o_ref, c_ref, n_ref, m_ref, ext_scr):
    h = pl.program_id(1)
    c = pl.program_id(2)
    bs, L = qpre_ref.shape[0], qpre_ref.shape[1]

    @pl.when(c == 0)
    def _():
        c_ref[...] = c0_ref[...]
        n_ref[...] = n0_ref[...]
        m_ref[...] = m0_ref[...]
        ext_scr[:, 0, 0:8, :] = cq_ref[...]
        ext_scr[:, 1, 0:8, :] = ck_ref[...]

    b_i = bi_ref[h]
    b_f = bf_ref[h]
    ti = lax.broadcasted_iota(jnp.int32, (L, L), 0)
    si = lax.broadcasted_iota(jnp.int32, (L, L), 1)
    tri = si <= ti

    def front(bi):
        ext_scr[bi, 0, 8:8 + L, :] = qpre_ref[bi].astype(F32)
        ext_scr[bi, 1, 8:8 + L, :] = kpre_ref[bi].astype(F32)

        def conv(idx, w_ref, b_ref):
            y = b_ref[...]
            for j in range(ML_CONV):
                lo = 8 - (ML_CONV - 1) + j
                y = y + ext_scr[bi, idx, lo:lo + L, :] * w_ref[j:j + 1, :]
            return y

        qc = conv(0, cwq_ref, cbq_ref)
        kc = conv(1, cwk_ref, cbk_ref)
        tail_q = ext_scr[bi, 0, L:L + 8, :]
        tail_k = ext_scr[bi, 1, L:L + 8, :]
        ext_scr[bi, 0, 0:8, :] = tail_q
        ext_scr[bi, 1, 0:8, :] = tail_k

        q = _silu(qc)
        k = _silu(kc) * (ML_DK ** -0.5)
        qb = q.astype(BF16)
        vb = v_ref[bi].astype(BF16)

        gc = gc_ref[bi, 0]
        gr = gr_ref[bi, 0]
        i_col = gc[:, 0:1] + b_i
        f_col = _log_sigmoid(gc[:, 1:2] + b_f)
        i_row = gr[0:1, :] + b_i
        f_row = _log_sigmoid(gr[1:2, :] + b_f)

        bcum_col = jnp.sum(jnp.where(tri, f_row, 0.0), axis=1, keepdims=True)
        bcum_row = jnp.sum(jnp.where(ti <= si, f_col, 0.0), axis=0, keepdims=True)
        b_last = jnp.sum(f_row, axis=1, keepdims=True)
        m_prev = m_ref[bi, 0]
        cmat = c_ref[bi, 0]
        nrow = n_ref[bi, 0]

        dec_col = b_last - bcum_col + i_col
        dec_row = b_last - bcum_row + i_row
        m_new = jnp.maximum(b_last + m_prev, jnp.max(dec_row, axis=1, keepdims=True))
        wk_col = jnp.exp(dec_col - m_new)
        decay = jnp.exp(b_last + m_prev - m_new)
        kw = wk_col * k

        qk = lax.dot_general(qb, k.astype(BF16), (((1,), (1,)), ((), ())), preferred_element_type=F32)
        q_c = jnp.dot(qb, cmat.astype(BF16), preferred_element_type=F32)
        kv = lax.dot_general(kw.astype(BF16), vb, (((0,), (0,)), ((), ())), preferred_element_type=F32)
        c_ref[bi, 0] = decay * cmat + kv
        n_ref[bi, 0] = decay * nrow + jnp.sum(kw, axis=0, keepdims=True)
        m_ref[bi, 0] = m_new
        dmat = jnp.where(tri, bcum_col - bcum_row + i_row, NEG)
        inter = bcum_col + m_prev
        q_n = jnp.sum(q * nrow, axis=1, keepdims=True)
        return qk, q_c, q_n, dmat, inter, vb

    def back(bi, vals):
        qk, q_c, q_n, dmat, inter, vb = vals
        mt = jnp.maximum(inter, jnp.max(dmat, axis=1, keepdims=True))
        s = qk * jnp.exp(dmat - mt)
        w_in = jnp.exp(inter - mt)
        num = jnp.dot(s.astype(BF16), vb, preferred_element_type=F32) + w_in * q_c
        den = jnp.sum(s, axis=1, keepdims=True) + w_in * q_n
        hh = num / jnp.maximum(jnp.abs(den), jnp.exp(-mt))

        mu = jnp.mean(hh, axis=1, keepdims=True)
        xc = hh - mu
        var = jnp.mean(xc * xc, axis=1, keepdims=True)
        hn = xc * lax.rsqrt(var + LN_EPS) * g_ref[...]
        o_ref[bi] = (_sigmoid(mo_ref[bi].astype(F32)) * hn).astype(o_ref.dtype)

    nxt = front(0)
    for bi in range(bs):
        cur = nxt
        if bi + 1 < bs:
            nxt = front(bi + 1)
        back(bi, cur)


def _mlstm(p3, gates_col, gates_row, conv8, conv_w, conv_b, norm_g, b_i, b_f, c0, n0, m0, L, bs):
    b, t, _ = p3.shape
    nc = t // L
    assert bs == 1 or nc == 1
    smem = pl.BlockSpec(memory_space=pltpu.SMEM)
    qb0 = COL_MQK // ML_DK
    kb0 = (COL_MQK + ML_QK_W) // ML_DK
    vb0 = COL_MV // ML_DV
    ob0 = COL_MO // ML_DV
    kq0 = ML_QK_W // ML_DK
    return pl.pallas_call(
        _mlstm_kernel,
        grid=(b // bs, ML_H, nc),
        in_specs=[smem, smem,
                  pl.BlockSpec((bs, L, ML_DK), lambda i, h, c: (i, c, qb0 + h)),
                  pl.BlockSpec((bs, L, ML_DK), lambda i, h, c: (i, c, kb0 + h)),
                  pl.BlockSpec((bs, L, ML_DV), lambda i, h, c: (i, c, vb0 + h)),
                  pl.BlockSpec((bs, L, ML_DV), lambda i, h, c: (i, c, ob0 + h)),
                  pl.BlockSpec((bs, 1, L, 2), lambda i, h, c: (i * nc + c, h, 0, 0)),
                  pl.BlockSpec((bs, 1, 2, L), lambda i, h, c: (i * nc + c, h, 0, 0)),
                  pl.BlockSpec((bs, 8, ML_DK), lambda i, h, c: (i, 0, h)),
                  pl.BlockSpec((bs, 8, ML_DK), lambda i, h, c: (i, 0, kq0 + h)),
                  pl.BlockSpec((ML_CONV, ML_DK), lambda i, h, c: (0, h)),
                  pl.BlockSpec((ML_CONV, ML_DK), lambda i, h, c: (0, kq0 + h)),
                  pl.BlockSpec((1, ML_DK), lambda i, h, c: (0, h)),
                  pl.BlockSpec((1, ML_DK), lambda i, h, c: (0, kq0 + h)),
                  pl.BlockSpec((1, ML_DV), lambda i, h, c: (0, h)),
                  pl.BlockSpec((bs, 1, ML_DK, ML_DV), lambda i, h, c: (i, h, 0, 0)),
                  pl.BlockSpec((bs, 1, 1, ML_DK), lambda i, h, c: (i, h, 0, 0)),
                  pl.BlockSpec((bs, 1, 1, 1), lambda i, h, c: (i, h, 0, 0))],
        out_specs=[pl.BlockSpec((bs, L, ML_DV), lambda i, h, c: (i, c, h)),
                   pl.BlockSpec((bs, 1, ML_DK, ML_DV), lambda i, h, c: (i, h, 0, 0)),
                   pl.BlockSpec((bs, 1, 1, ML_DK), lambda i, h, c: (i, h, 0, 0)),
                   pl.BlockSpec((bs, 1, 1, 1), lambda i, h, c: (i, h, 0, 0))],
        out_shape=[jax.ShapeDtypeStruct((b, t, ML_V_W), F32),
                   jax.ShapeDtypeStruct((b, ML_H, ML_DK, ML_DV), F32),
                   jax.ShapeDtypeStruct((b, ML_H, 1, ML_DK), F32),
                   jax.ShapeDtypeStruct((b, ML_H, 1, 1), F32)],
        scratch_shapes=[pltpu.VMEM((bs, 2, L + 8, ML_DK), F32)],
        compiler_params=_cparams(("parallel", "parallel", "arbitrary")),
        name="mlstm",
    )(b_i, b_f, p3, p3, p3, p3, gates_col, gates_row, conv8, conv8, conv_w, conv_w,
      conv_b, conv_b, norm_g, c0, n0, m0)


def _layer_norm(r, g, b):
    mu = jnp.mean(r, axis=-1, keepdims=True)
    xc = r - mu
    var = jnp.mean(xc * xc, axis=-1, keepdims=True)
    return xc * lax.rsqrt(var + LN_EPS) * g + b


def _mix_kernel(att_ref, hm_ref, ga0_ref, ga1_ref, gb0_ref, gb1_ref, x_ref, g1_ref, sc2_ref, sh2_ref,
                wa_ref, wb_ref, wo_ref, lg_ref, lb_ref, x1_ref, h2t_ref):
    ya = jnp.dot(att_ref[...].astype(BF16), wa_ref[...], preferred_element_type=F32)
    yb = jnp.dot(hm_ref[...].astype(BF16), wb_ref[...], preferred_element_type=F32)
    ga = jnp.concatenate([ga0_ref[...], ga1_ref[...]], axis=1).astype(F32)
    gb = jnp.concatenate([gb0_ref[...], gb1_ref[...]], axis=1).astype(F32)
    z = _sigmoid(ga) * ya + _sigmoid(gb) * yb
    mix = jnp.dot(z.astype(BF16), wo_ref[...], preferred_element_type=F32)
    shp = x_ref.shape
    r = ALPHA * x_ref[...] + g1_ref[...] * mix.reshape(shp)
    x1 = _layer_norm(r, lg_ref[...], lb_ref[...])
    x1_ref[...] = x1
    h2 = x1 * (1.0 + sc2_ref[...]) + sh2_ref[...]
    h2t_ref[...] = h2.reshape(mix.shape).T.astype(BF16)


def _mix(att2, hm2, p2, x, g1, sc2, sh2, wa, wb, wo, ln_g, ln_b, bb, tt):
    bx, tx, d = x.shape
    tm = bb * tt
    ntok = bx * tx
    tpb = tx // tt
    xmap = lambda i: (i // tpb if bb == 1 else i, i % tpb if bb == 1 else 0, 0)
    mmap = lambda i: (i // tpb if bb == 1 else i, 0, 0)
    hw = d // 2
    wspec = lambda shp: pl.BlockSpec(shp, lambda i: (0, 0), pipeline_mode=pl.Buffered(1))
    return pl.pallas_call(
        _mix_kernel,
        grid=(ntok // tm,),
        in_specs=[pl.BlockSpec((tm, d), lambda i: (i, 0)),
                  pl.BlockSpec((tm, d), lambda i: (i, 0)),
                  pl.BlockSpec((tm, hw), lambda i: (i, COL_GA // hw)),
                  pl.BlockSpec((tm, hw), lambda i: (i, COL_GA // hw + 1)),
                  pl.BlockSpec((tm, hw), lambda i: (i, COL_GB // hw)),
                  pl.BlockSpec((tm, hw), lambda i: (i, COL_GB // hw + 1)),
                  pl.BlockSpec((bb, tt, d), xmap),
                  pl.BlockSpec((bb, 1, d), mmap),
                  pl.BlockSpec((bb, 1, d), mmap),
                  pl.BlockSpec((bb, 1, d), mmap),
                  wspec(wa.shape), wspec(wb.shape), wspec(wo.shape),
                  pl.BlockSpec((1, d), lambda i: (0, 0)),
                  pl.BlockSpec((1, d), lambda i: (0, 0))],
        out_specs=[pl.BlockSpec((bb, tt, d), xmap),
                   pl.BlockSpec((d, tm), lambda i: (0, i))],
        out_shape=[jax.ShapeDtypeStruct((bx, tx, d), F32),
                   jax.ShapeDtypeStruct((d, ntok), BF16)],
        compiler_params=_cparams(("parallel",)),
        name="mix",
    )(att2, hm2, p2, p2, p2, p2, x, g1, sc2, sh2, wa, wb, wo, ln_g, ln_b)


def _top16(val, kio):
    work = val
    rank = jnp.full(val.shape, float(TOPK), F32)
    tops = []
    nkeys = val.shape[0]
    for r in range(TOPK):
        m = jnp.max(work, axis=0, keepdims=True)
        idx = jnp.min(jnp.where(work == m, kio, float(nkeys)), axis=0, keepdims=True)
        hit = kio == idx
        rank = jnp.where(hit, float(r), rank)
        work = jnp.where(hit, -jnp.inf, work)
        tops.append(m)
    return rank, tops


def _top16_distinct(val, want_rank):
    work = val
    rank = jnp.full(val.shape, float(TOPK), F32) if want_rank else None
    tops = []
    for r in range(TOPK):
        m = jnp.max(work, axis=0, keepdims=True)
        hit = work == m
        if want_rank:
            rank = jnp.where(hit, float(r), rank)
        work = jnp.where(hit, -jnp.inf, work)
        tops.append(m)
    removed = jnp.sum(jnp.where(work == -jnp.inf, 1.0, 0.0), axis=0, keepdims=True)
    return rank, tops, removed


def _peer_select_kernel(h2t_ref, wq_ref, keys_ref, rb_ref, eb_ref, cnt_ref, ea_ref):
    tl = h2t_ref.shape[1]
    kio = lax.broadcasted_iota(jnp.int32, (NK, tl), 0).astype(F32)
    rio = lax.broadcasted_iota(jnp.int32, (TOPK, tl), 0).astype(F32)

    def scores(hd):
        qp = jnp.dot(wq_ref[hd * 2 * NK:(hd + 1) * 2 * NK, :], h2t_ref[...],
                     preferred_element_type=F32).astype(BF16)
        return (jnp.dot(keys_ref[2 * hd], qp[:NK], preferred_element_type=F32),
                jnp.dot(keys_ref[2 * hd + 1], qp[NK:], preferred_element_type=F32))

    def head(hd, a, b, exact):
        if exact:
            ra, atop = _top16(a, kio)
            rb, btop = _top16(b, kio)
            tied = None
        else:
            _, atop, na = _top16_distinct(a, False)
            rb, btop, nb = _top16_distinct(b, True)
            tied = jnp.max(jnp.maximum(na, nb)) > float(TOPK)
        asort = jnp.concatenate(atop, axis=0)
        cnt = jnp.zeros((TOPK, tl), F32)
        front = asort + btop[0]
        for _ in range(TOPK):
            m = jnp.max(front, axis=0, keepdims=True)
            idx = jnp.min(jnp.where(front == m, rio, float(TOPK)), axis=0, keepdims=True)
            hit = rio == idx
            cnt = cnt + jnp.where(hit, 1.0, 0.0)
            nxt = jnp.full((TOPK, tl), -jnp.inf, F32)
            for cc in range(1, TOPK):
                nxt = jnp.where(cnt == float(cc), btop[cc], nxt)
            front = jnp.where(hit, asort + nxt, front)
        ea_s = jnp.exp(asort - atop[0])
        pref = jnp.zeros((1, tl), F32)
        pbsel = jnp.zeros((TOPK, tl), F32)
        for cc in range(1, TOPK + 1):
            pref = pref + jnp.exp(btop[cc - 1] - btop[0])
            pbsel = jnp.where(cnt == float(cc), pref, pbsel)
        z = jnp.sum(ea_s * pbsel, axis=0, keepdims=True)
        cnti = jnp.zeros((NK, tl), F32)
        for r in range(TOPK):
            match = (ra == float(r)) if exact else (a == atop[r])
            cnti = jnp.where(match, cnt[r:r + 1, :], cnti)
        rb_ref[hd] = rb.astype(rb_ref.dtype)
        eb_ref[hd] = jnp.exp(b - btop[0]).astype(eb_ref.dtype)
        cnt_ref[hd] = cnti
        ea_ref[hd] = jnp.exp(a - atop[0]) * (0.5 / z)
        return tied

    tied = []
    nxt = scores(0)
    for hd in range(PEER_H):
        a, b = nxt
        if hd + 1 < PEER_H:
            nxt = scores(hd + 1)
        tied.append(head(hd, a, b, False))

    for hd in range(PEER_H):
        @pl.when(tied[hd])
        def _(hd=hd):
            head(hd, *scores(hd), True)


def _peer_select(h2t, wqt, keys, tl):
    d, ntok = h2t.shape
    out = jax.ShapeDtypeStruct((PEER_H, NK, ntok), F32)
    outb = jax.ShapeDtypeStruct((PEER_H, NK, ntok), jnp.bfloat16)
    ospec = pl.BlockSpec((PEER_H, NK, tl), lambda i: (0, 0, i))
    return pl.pallas_call(
        _peer_select_kernel,
        grid=(ntok // tl,),
        in_specs=[pl.BlockSpec((d, tl), lambda i: (0, i)),
                  pl.BlockSpec(wqt.shape, lambda i: (0, 0), pipeline_mode=pl.Buffered(1)),
                  pl.BlockSpec(keys.shape, lambda i: (0, 0, 0))],
        out_specs=[ospec, ospec, ospec, ospec],
        out_shape=[outb, outb, out, out],
        compiler_params=_cparams(("parallel",)),
        name="peer_select",
    )(h2t, wqt, keys)


def _peer_dense_kernel(h2t_ref, u_ref, vt_ref, rb_ref, eb_ref, cnt_ref, ea_ref, o_ref,
                       act0_ref, act1_ref, mt_ref, *, n_e):
    s = pl.program_id(0)
    e_prev = jnp.maximum(s - 1, 0) % n_e
    half = u_ref.shape[0] // 2
    zero = jnp.zeros((), rb_ref.dtype)

    @pl.when(s == 0)
    def _():
        act1_ref[...] = jnp.zeros_like(act1_ref)

    @pl.when(e_prev == 0)
    def _():
        o_ref[...] = jnp.zeros_like(o_ref)

    tm = h2t_ref.shape[1]
    pk = rb_ref.shape[2]

    def row(ref, hd, il):
        return jnp.broadcast_to(ref[hd, il:il + 1, :], (pk, tm)).astype(rb_ref.dtype)[None]

    def gated(old_ref, hf):
        for il in range(hf * half // NK, (hf + 1) * half // NK):
            gate = None
            for hd in range(PEER_H):
                term = jnp.where(rb_ref[hd] < row(cnt_ref, hd, il), eb_ref[hd] * row(ea_ref, hd, il), zero)
                gate = term if gate is None else gate + term
            rows = slice(il * NK, (il + 1) * NK)
            a = old_ref[rows, :]
            half_gelu2 = a * (1.0 + lax.erf(a * (2.0 ** -0.5)))
            mt_ref[rows, :] = (gate.reshape(NK, tm) * half_gelu2.astype(gate.dtype)).astype(mt_ref.dtype)

    def phases(new_ref, old_ref):
        new_ref[...] = jnp.dot(u_ref[...], h2t_ref[...], preferred_element_type=F32)
        for hf in range(2):
            rows = slice(hf * half, (hf + 1) * half)
            gated(old_ref, hf)
            o_ref[...] += jnp.dot(vt_ref[:, rows], mt_ref[rows, :], preferred_element_type=F32)

    @pl.when(s % 2 == 0)
    def _():
        phases(act0_ref, act1_ref)

    @pl.when(s % 2 == 1)
    def _():
        phases(act1_ref, act0_ref)


def _peer_dense(h2t, u, vt, rb, eb, cnti, ea, tm):
    d, ntok = h2t.shape
    et = 8 * NK
    n_e = u.shape[0] // et
    n_items = (ntok // tm) * n_e
    item_a = lambda s: jnp.minimum(s, n_items - 1)
    item_b = lambda s: jnp.maximum(s - 1, 0)
    pk = 16
    rb = rb.reshape(PEER_H, NK // pk, pk, ntok)
    eb = eb.reshape(PEER_H, NK // pk, pk, ntok)
    rspec = pl.BlockSpec((PEER_H, NK // pk, pk, tm), lambda s: (0, 0, 0, item_b(s) // n_e))
    cspec = pl.BlockSpec((PEER_H, 8, tm), lambda s: (0, item_b(s) % n_e, item_b(s) // n_e))
    return pl.pallas_call(
        functools.partial(_peer_dense_kernel, n_e=n_e),
        grid=(n_items + 1,),
        in_specs=[pl.BlockSpec((d, tm), lambda s: (0, item_a(s) // n_e)),
                  pl.BlockSpec((et, d), lambda s: (item_a(s) % n_e, 0)),
                  pl.BlockSpec((d, et), lambda s: (0, item_b(s) % n_e)),
                  rspec, rspec, cspec, cspec],
        out_specs=pl.BlockSpec((d, tm), lambda s: (0, item_b(s) // n_e)),
        out_shape=jax.ShapeDtypeStruct((d, ntok), F32),
        scratch_shapes=[pltpu.VMEM((et, tm), F32), pltpu.VMEM((et, tm), F32),
                        pltpu.VMEM((et, tm), BF16)],
        compiler_params=_cparams(("arbitrary",)),
        name="peer_dense",
    )(h2t, u, vt, rb, eb, cnti, ea)


def _final_kernel(x1_ref, po_ref, g2_ref, lg_ref, lb_ref, y_ref):
    shp = x1_ref.shape
    r = ALPHA * x1_ref[...] + g2_ref[...] * po_ref[...].T.reshape(shp)
    y_ref[...] = _layer_norm(r, lg_ref[...], lb_ref[...])


def _final(x1, po, g2, ln_g, ln_b, bb, tt):
    bx, tx, d = x1.shape
    tm = bb * tt
    tpb = tx // tt
    xmap = lambda i: (i // tpb if bb == 1 else i, i % tpb if bb == 1 else 0, 0)
    mmap = lambda i: (i // tpb if bb == 1 else i, 0, 0)
    return pl.pallas_call(
        _final_kernel,
        grid=(bx * tx // tm,),
        in_specs=[pl.BlockSpec((bb, tt, d), xmap),
                  pl.BlockSpec((d, tm), lambda i: (0, i)),
                  pl.BlockSpec((bb, 1, d), mmap),
                  pl.BlockSpec((1, d), lambda i: (0, 0)),
                  pl.BlockSpec((1, d), lambda i: (0, 0))],
        out_specs=pl.BlockSpec((bb, tt, d), xmap),
        out_shape=jax.ShapeDtypeStruct((bx, tx, d), F32),
        compiler_params=_cparams(("parallel",)),
        name="final_ln",
    )(x1, po, g2, ln_g, ln_b)


def _tile(bx, tx, want):
    if tx >= want:
        return 1, want
    return want // tx, tx


def _layer(x, ada, state, prm):
    bx, tx, d = x.shape
    ntok = bx * tx
    sh1, sc1, g1, sh2, sc2, g2 = [a[:, None, :] for a in jnp.split(ada, 6, axis=-1)]

    bb, tt = _tile(bx, tx, min(1024, ntok))
    p2, gates2 = _inproj(x, sc1, sh1, prm["w_in1"], prm["w_in2"], prm["w_if"], bb, tt)
    p3 = p2.reshape(bx, tx, IN_MAIN)

    if state is None:
        att = _attn_prompt(p3, prm["bias_prompt"], prm["sink_prompt"])
        keep = min(WINDOW, tx)
        k_keep = p3[:, tx - keep:, COL_AK:COL_AK + ATT_KV_W].astype(F32)
        v_keep = p3[:, tx - keep:, COL_AV:COL_AV + ATT_KV_W].astype(F32)
        conv_buf = jnp.zeros((bx, ML_CONV - 1, 2 * ML_QK_W), F32)
        c0 = jnp.zeros((bx, ML_H, ML_DK, ML_DV), F32)
        n0 = jnp.zeros((bx, ML_H, ML_DK), F32)
        m0 = jnp.zeros((bx, ML_H), F32)
    else:
        kwin, vwin, conv_buf, c0, n0, m0 = state
        wc = kwin.shape[1]
        att, k_keep, v_keep = _attn_sample(
            p3, kwin.reshape(bx, wc, ATT_KV_W), vwin.reshape(bx, wc, ATT_KV_W),
            prm["bias_sample_w"], prm["bias_sample_n"], prm["sink_sample"], math.gcd(bx, 8))
    k_keep = k_keep.reshape(bx, -1, ATT_KV, HD)
    v_keep = v_keep.reshape(bx, -1, ATT_KV, HD)

    L = min(ML_CHUNK_MAX, tx)
    nc = tx // L
    gates = gates2[:, :2 * ML_H].reshape(bx * nc, L, 2, ML_H)
    gates_col = jnp.transpose(gates, (0, 3, 1, 2))
    gates_row = jnp.transpose(gates, (0, 3, 2, 1))
    conv8 = jnp.pad(conv_buf.astype(F32), ((0, 0), (8 - (ML_CONV - 1), 0), (0, 0)))
    hm, c1, n1, m1 = _mlstm(p3, gates_col, gates_row, conv8, prm["conv_w"], prm["conv_b"],
                            prm["norm_g"], prm["b_i"], prm["b_f"], c0.astype(F32),
                            n0.astype(F32).reshape(bx, ML_H, 1, ML_DK),
                            m0.astype(F32).reshape(bx, ML_H, 1, 1), L,
                            math.gcd(bx, 8) if nc == 1 else 1)
    conv_keep = p3[:, tx - (ML_CONV - 1):, COL_MQK:COL_MQK + 2 * ML_QK_W].astype(F32)
    new_state = (k_keep, v_keep, conv_keep, c1, n1.reshape(bx, ML_H, ML_DK), m1.reshape(bx, ML_H))

    bb, tt = _tile(bx, tx, min(256, ntok))
    x1, h2t = _mix(att.reshape(ntok, ATT_Q_W), hm.reshape(ntok, ML_V_W), p2, x, g1, sc2, sh2,
                   prm["w_a"], prm["w_b"], prm["w_out"], prm["ln1_g"], prm["ln1_b"], bb, tt)

    rb, eb, cnti, ea = _peer_select(h2t, prm["w_pqt"], prm["keys"], min(256, ntok))
    po = _peer_dense(h2t, prm["peer_u"], prm["peer_vt"], rb, eb, cnti, ea, min(512, ntok))

    bb, tt = _tile(bx, tx, min(512, ntok))
    y = _final(x1, po, g2, prm["ln2_g"], prm["ln2_b"], bb, tt)
    return y, new_state


def kernel(x_prompt, x_sample, c_prompt, c_sample, cache_k_win, cache_v_win, state_conv, state_C, state_n, state_m, rel_bias_table, w_ada, b_ada, w_in, b_i, b_f, conv_w, conv_b, ml_norm_g, att_sinks, w_a, w_b, w_out, ln1_g, ln1_b, w_pq, peer_sub_keys, peer_u, peer_v, ln2_g, ln2_b):
    assert w_ada.shape[0] == DEPTH
    d = x_prompt.shape[-1]
    bp = x_prompt.shape[0]
    ts = x_sample.shape[1]
    wc = cache_k_win.shape[2]

    wi = w_in[0]
    n_if = 2 * ML_H
    assert wi.shape[1] == IN_MAIN + n_if
    w_in1 = wi[:, :IN_W1].astype(BF16)
    w_in2 = wi[:, IN_W1 + n_if:].astype(BF16)
    w_if = jnp.pad(wi[:, IN_W1:IN_W1 + n_if], ((0, 0), (0, IF_PAD - n_if))).astype(BF16)

    sinks = att_sinks[0].astype(F32).reshape(ATT_KV, ATT_G, 1, 1)
    blk = WINDOW
    dist_p = (jnp.arange(blk)[:, None] + blk) - jnp.arange(2 * blk)[None, :]
    dist_s = (jnp.arange(ts)[:, None] + wc) - jnp.arange(wc + ts)[None, :]
    bias_s = _bias_table(rel_bias_table, dist_s)
    bias_p = _bias_table_t(rel_bias_table, dist_p)
    prm = {
        "w_in1": w_in1, "w_in2": w_in2, "w_if": w_if,
        "bias_prompt": jnp.stack([jnp.where(jnp.arange(2 * blk)[None, :, None] < blk, NEG, bias_p), bias_p]),
        "sink_prompt": jnp.broadcast_to(att_sinks[0].astype(F32)[:, None, None], (ATT_HEADS, 1, blk)),
        "bias_sample_w": bias_s[:, :, :wc],
        "bias_sample_n": bias_s[:, :, wc:],
        "sink_sample": jnp.broadcast_to(sinks, (ATT_KV, ATT_G, ts, 1)).reshape(ATT_KV, ATT_G * ts, 1),
        "conv_w": conv_w[0], "conv_b": conv_b[0].reshape(1, -1), "norm_g": ml_norm_g[0].reshape(1, -1),
        "b_i": b_i[0], "b_f": b_f[0],
        "w_a": w_a[0].astype(BF16), "w_b": w_b[0].astype(BF16), "w_out": w_out[0].astype(BF16),
        "ln1_g": ln1_g[0].reshape(1, -1), "ln1_b": ln1_b[0].reshape(1, -1),
        "w_pqt": w_pq[0].T.astype(BF16),
        "keys": peer_sub_keys[0].reshape(2 * PEER_H, NK, -1).astype(BF16),
        "peer_u": peer_u[0].astype(BF16),
        "peer_vt": peer_v[0].T.astype(BF16),
        "ln2_g": ln2_g[0].reshape(1, -1), "ln2_b": ln2_b[0].reshape(1, -1),
    }

    ada = _ada(jnp.concatenate([c_prompt, c_sample], axis=0), w_ada[0], b_ada[0])
    yp, sp = _layer(x_prompt, ada[:bp], None, prm)
    ys, ss = _layer(x_sample, ada[bp:],
                    (cache_k_win[0], cache_v_win[0], state_conv[0], state_C[0], state_n[0], state_m[0]), prm)
    stack = lambda s: [a[None] for a in s]
    return (yp, ys, *stack(sp), *stack(ss))
```

```python
import functools
import math

import jax
import jax.numpy as jnp
from jax import lax
from jax.experimental import pallas as pl
from jax.experimental.pallas import tpu as pltpu

F32 = jnp.float32
BF16 = jnp.bfloat16

ATT_HEADS = 32
ATT_KV = 8
ATT_G = ATT_HEADS // ATT_KV
HD = 64
WINDOW = 128
N_BUCKETS = 32
MAX_DISTANCE = 128
ML_H = 4
ML_DK = 256
ML_DV = 512
ML_CONV = 4
PEER_H = 8
NK = 128
TOPK = 16
ATT_Q_W = ATT_HEADS * HD
ATT_KV_W = ATT_KV * HD
ML_QK_W = ML_H * ML_DK
ML_V_W = ML_H * ML_DV
DEPTH = 1
ALPHA = (2.0 * DEPTH) ** 0.25
LN_EPS = 1e-5
NEG = -1e30

COL_AQ = 0
COL_AK = 2048
COL_AV = 2560
COL_MQK = 3072
COL_MV = 5120
IN_W1 = 7168
COL_MO = 7168
COL_GA = 9216
COL_GB = 11264
IN_MAIN = 13312
IN_TN = 1024
IF_PAD = 128

ML_CHUNK_MAX = 256
VMEM_LIMIT = 56 * 1024 * 1024


def _cparams(sem):
    return pltpu.CompilerParams(dimension_semantics=sem, vmem_limit_bytes=VMEM_LIMIT)


def _sigmoid(x):
    return 1.0 / (1.0 + jnp.exp(-x))


def _silu(x):
    return x * _sigmoid(x)


def _log_sigmoid(x):
    return jnp.minimum(x, 0.0) - jnp.log1p(jnp.exp(-jnp.abs(x)))


def _ada_kernel(c_ref, w_ref, b_ref, o_ref):
    c = c_ref[...]
    a = _silu(c).astype(BF16)
    o_ref[...] = jnp.dot(a, w_ref[...].astype(BF16), preferred_element_type=F32) + b_ref[...]


def _ada(c_all, w_ada, b_ada):
    r, d = c_all.shape
    n = w_ada.shape[1]
    tn = 1024
    return pl.pallas_call(
        _ada_kernel,
        grid=(n // tn,),
        in_specs=[pl.BlockSpec((r, d), lambda j: (0, 0)),
                  pl.BlockSpec((d, tn), lambda j: (0, j)),
                  pl.BlockSpec((1, tn), lambda j: (0, j))],
        out_specs=pl.BlockSpec((r, tn), lambda j: (0, j)),
        out_shape=jax.ShapeDtypeStruct((r, n), F32),
        compiler_params=_cparams(("parallel",)),
        name="ada",
    )(c_all, w_ada, b_ada.reshape(1, n))


def _inproj_kernel(x_ref, sc_ref, sh_ref, w1_ref, w2_ref, wif_ref, p_ref, if_ref, h_scr, *, n1):
    j = pl.program_id(1)

    @pl.when(j == 0)
    def _():
        h = x_ref[...] * (1.0 + sc_ref[...]) + sh_ref[...]
        hb = h.reshape(h_scr.shape).astype(BF16)
        h_scr[...] = hb
        if_ref[...] = jnp.dot(hb, wif_ref[...], preferred_element_type=F32)

    @pl.when(j < n1)
    def _():
        p_ref[...] = jnp.dot(h_scr[...], w1_ref[...], preferred_element_type=F32).astype(p_ref.dtype)

    @pl.when(j >= n1)
    def _():
        p_ref[...] = jnp.dot(h_scr[...], w2_ref[...], preferred_element_type=F32).astype(p_ref.dtype)


def _inproj(x, sc, sh, w1, w2, wif, bb, tt):
    bx, tx, d = x.shape
    tm = bb * tt
    ntok = bx * tx
    n1 = IN_W1 // IN_TN
    n2 = w2.shape[1] // IN_TN
    tpb = tx // tt
    xmap = lambda i, j: (i // tpb if bb == 1 else i, i % tpb if bb == 1 else 0, 0)
    mmap = lambda i, j: (i // tpb if bb == 1 else i, 0, 0)
    return pl.pallas_call(
        functools.partial(_inproj_kernel, n1=n1),
        grid=(ntok // tm, n1 + n2),
        in_specs=[pl.BlockSpec((bb, tt, d), xmap),
                  pl.BlockSpec((bb, 1, d), mmap),
                  pl.BlockSpec((bb, 1, d), mmap),
                  pl.BlockSpec((d, IN_TN), lambda i, j: (0, jnp.minimum(j, n1 - 1))),
                  pl.BlockSpec((d, IN_TN), lambda i, j: (0, jnp.maximum(j - n1, 0))),
                  pl.BlockSpec((d, IF_PAD), lambda i, j: (0, 0))],
        out_specs=[pl.BlockSpec((tm, IN_TN), lambda i, j: (i, j)),
                   pl.BlockSpec((tm, IF_PAD), lambda i, j: (i, 0))],
        out_shape=[jax.ShapeDtypeStruct((ntok, IN_MAIN), BF16),
                   jax.ShapeDtypeStruct((ntok, IF_PAD), F32)],
        scratch_shapes=[pltpu.VMEM((tm, d), BF16)],
        compiler_params=_cparams(("parallel", "arbitrary")),
        name="inproj",
    )(x, sc, sh, w1, w2, wif)


def _t5_bucket(dist):
    max_exact = N_BUCKETS // 2
    d = jnp.maximum(dist, 0)
    dl = jnp.maximum(d, max_exact).astype(F32)
    large = max_exact + (jnp.log(dl / max_exact) / math.log(MAX_DISTANCE / max_exact)
                         * (N_BUCKETS - max_exact)).astype(jnp.int32)
    large = jnp.minimum(large, N_BUCKETS - 1)
    return jnp.where(d < max_exact, d, large)


def _bias_table(rel_table, dist):
    lq, lk = dist.shape
    band = (dist >= 0) & (dist < WINDOW)
    bias = jnp.transpose(rel_table[_t5_bucket(dist)], (2, 0, 1)).astype(F32)
    bias = jnp.where(band[None], bias, NEG)
    return bias.reshape(ATT_KV, ATT_G * lq, lk)


def _bias_table_t(rel_table, dist):
    lq, lk = dist.shape
    bucket = _t5_bucket(dist).T.reshape(1, lk * lq)
    onehot = (bucket == jnp.arange(N_BUCKETS)[:, None]).astype(F32)
    bias = jnp.dot(rel_table.T.astype(F32), onehot, precision=lax.Precision.HIGHEST)
    band = ((dist >= 0) & (dist < WINDOW)).T.reshape(1, lk * lq)
    return jnp.where(band, bias, NEG).reshape(-1, lk, lq)


def _attn_prompt_kernel(q_ref, kp_ref, kc_ref, vp_ref, vc_ref, bias_ref, sink_ref, o_ref):
    qt = (q_ref[0] * (HD ** -0.5)).T.astype(BF16)
    kcat = jnp.concatenate([kp_ref[0], kc_ref[0]], axis=0).astype(BF16)
    vt = jnp.concatenate([vp_ref[0], vc_ref[0]], axis=0).T.astype(BF16)

    def scores(kv):
        kh = kcat[:, kv * HD:(kv + 1) * HD]
        return [jnp.dot(kh, qt[h * HD:(h + 1) * HD, :], preferred_element_type=F32) + bias_ref[0, h]
                for h in range(kv * ATT_G, (kv + 1) * ATT_G)]

    def finish(kv, sts):
        vth = vt[kv * HD:(kv + 1) * HD, :]
        ps, dens = [], []
        for g, st in enumerate(sts):
            sink = sink_ref[kv * ATT_G + g]
            m = jnp.maximum(jnp.max(st, axis=0, keepdims=True), sink)
            p = jnp.exp(st - m)
            dens.append(jnp.sum(p, axis=0, keepdims=True) + jnp.exp(sink - m))
            ps.append(p.astype(BF16))
        outs = [jnp.dot(vth, p, preferred_element_type=F32) / den for p, den in zip(ps, dens)]
        for g in range(0, ATT_G, 2):
            h = kv * ATT_G + g
            o_ref[0, :, h * HD:(h + 2) * HD] = jnp.concatenate(outs[g:g + 2], axis=0).T.astype(o_ref.dtype)

    nxt = scores(0)
    for kv in range(ATT_KV):
        cur = nxt
        if kv + 1 < ATT_KV:
            nxt = scores(kv + 1)
        finish(kv, cur)


def _attn_prompt(p3, bias2, sinkrow):
    b, t, _ = p3.shape
    blk = WINDOW
    nb = t // blk
    kcol, vcol = COL_AK // ATT_KV_W, COL_AV // ATT_KV_W
    return pl.pallas_call(
        _attn_prompt_kernel,
        grid=(b, nb),
        in_specs=[pl.BlockSpec((1, blk, ATT_Q_W), lambda i, n: (i, n, 0)),
                  pl.BlockSpec((1, blk, ATT_KV_W), lambda i, n: (i, jnp.maximum(n - 1, 0), kcol)),
                  pl.BlockSpec((1, blk, ATT_KV_W), lambda i, n: (i, n, kcol)),
                  pl.BlockSpec((1, blk, ATT_KV_W), lambda i, n: (i, jnp.maximum(n - 1, 0), vcol)),
                  pl.BlockSpec((1, blk, ATT_KV_W), lambda i, n: (i, n, vcol)),
                  pl.BlockSpec((1,) + bias2.shape[1:], lambda i, n: (jnp.minimum(n, 1), 0, 0, 0)),
                  pl.BlockSpec(sinkrow.shape, lambda i, n: (0, 0, 0))],
        out_specs=pl.BlockSpec((1, blk, ATT_Q_W), lambda i, n: (i, n, 0)),
        out_shape=jax.ShapeDtypeStruct((b, t, ATT_Q_W), BF16),
        compiler_params=_cparams(("parallel", "arbitrary")),
        name="attn_prompt",
    )(p3, p3, p3, p3, p3, bias2, sinkrow)


def _attn_sample_kernel(q_ref, kn_ref, vn_ref, kw_ref, vw_ref, bw_ref, bn_ref, sink_ref,
                        o_ref, ko_ref, vo_ref):
    bs, t = q_ref.shape[0], q_ref.shape[1]
    wc = kw_ref.shape[1]
    nt = (((1,), (1,)), ((), ()))

    def scores(si):
        q = q_ref[si].astype(F32) * (HD ** -0.5)
        kwb = kw_ref[si].astype(BF16)
        knb = kn_ref[si].astype(BF16)
        out = []
        for kv in range(ATT_KV):
            qs = jnp.concatenate(
                [q[:, (kv * ATT_G + g) * HD:(kv * ATT_G + g + 1) * HD] for g in range(ATT_G)], axis=0
            ).astype(BF16)
            sl = slice(kv * HD, (kv + 1) * HD)
            out.append((lax.dot_general(qs, kwb[:, sl], nt, preferred_element_type=F32) + bw_ref[kv],
                        lax.dot_general(qs, knb[:, sl], nt, preferred_element_type=F32) + bn_ref[kv]))
        return out

    def finish(si, sts):
        kn = kn_ref[si]
        vn = vn_ref[si]
        kw = kw_ref[si]
        vw = vw_ref[si]
        ko_ref[si, 0:wc - t, :] = kw[t:, :]
        ko_ref[si, wc - t:wc, :] = kn.astype(F32)
        vo_ref[si, 0:wc - t, :] = vw[t:, :]
        vo_ref[si, wc - t:wc, :] = vn.astype(F32)
        vnb, vwb = vn.astype(BF16), vw.astype(BF16)
        probs = []
        for kv, (sw, sn) in enumerate(sts):
            sink = sink_ref[kv]
            m = jnp.maximum(jnp.maximum(jnp.max(sw, axis=-1, keepdims=True),
                                        jnp.max(sn, axis=-1, keepdims=True)), sink)
            pw = jnp.exp(sw - m)
            pn = jnp.exp(sn - m)
            den = (jnp.sum(pw, axis=-1, keepdims=True) + jnp.sum(pn, axis=-1, keepdims=True)
                   + jnp.exp(sink - m))
            probs.append((pw.astype(BF16), pn.astype(BF16), den))
        outs = []
        for kv, (pw, pn, den) in enumerate(probs):
            sl = slice(kv * HD, (kv + 1) * HD)
            o = (jnp.dot(pw, vwb[:, sl], preferred_element_type=F32)
                 + jnp.dot(pn, vnb[:, sl], preferred_element_type=F32)) / den
            for g in range(ATT_G):
                outs.append(o[g * t:(g + 1) * t])
        o_ref[si] = jnp.concatenate(outs, axis=-1).astype(o_ref.dtype)

    nxt = scores(0)
    for si in range(bs):
        cur = nxt
        if si + 1 < bs:
            nxt = scores(si + 1)
        finish(si, cur)


def _attn_sample(p3, kwin, vwin, bias_w, bias_n, sinkcol, bs):
    b, t, _ = p3.shape
    wc = kwin.shape[1]
    kcol, vcol = COL_AK // ATT_KV_W, COL_AV // ATT_KV_W
    seq3 = lambda i: (i, 0, 0)
    const3 = lambda i: (0, 0, 0)
    return pl.pallas_call(
        _attn_sample_kernel,
        grid=(b // bs,),
        in_specs=[pl.BlockSpec((bs, t, ATT_Q_W), seq3),
                  pl.BlockSpec((bs, t, ATT_KV_W), lambda i: (i, 0, kcol)),
                  pl.BlockSpec((bs, t, ATT_KV_W), lambda i: (i, 0, vcol)),
                  pl.BlockSpec((bs, wc, ATT_KV_W), seq3),
                  pl.BlockSpec((bs, wc, ATT_KV_W), seq3),
                  pl.BlockSpec(bias_w.shape, const3),
                  pl.BlockSpec(bias_n.shape, const3),
                  pl.BlockSpec(sinkcol.shape, const3)],
        out_specs=[pl.BlockSpec((bs, t, ATT_Q_W), seq3),
                   pl.BlockSpec((bs, wc, ATT_KV_W), seq3),
                   pl.BlockSpec((bs, wc, ATT_KV_W), seq3)],
        out_shape=[jax.ShapeDtypeStruct((b, t, ATT_Q_W), F32),
                   jax.ShapeDtypeStruct((b, wc, ATT_KV_W), F32),
                   jax.ShapeDtypeStruct((b, wc, ATT_KV_W), F32)],
        compiler_params=_cparams(("parallel",)),
        name="attn_sample",
    )(p3, p3, p3, kwin, vwin, bias_w, bias_n, sinkcol)


def _mlstm_kernel(bi_ref, bf_ref, qpre_ref, kpre_ref, v_ref, mo_ref, gc_ref, gr_ref,
                  cq_ref, ck_ref, cwq_ref, cwk_ref, cbq_ref, cbk_ref, g_ref,
                  c0_ref, n0_ref, m0_ref,
                  o_ref, c_ref, n_ref, m_ref, ext_scr):
    h = pl.program_id(1)
    c = pl.program_id(2)
    bs, L = qpre_ref.shape[0], qpre_ref.shape[1]

    @pl.when(c == 0)
    def _():
        c_ref[...] = c0_ref[...]
        n_ref[...] = n0_ref[...]
        m_ref[...] = m0_ref[...]
        ext_scr[:, 0, 0:8, :] = cq_ref[...]
        ext_scr[:, 1, 0:8, :] = ck_ref[...]

    b_i = bi_ref[h]
    b_f = bf_ref[h]
    ti = lax.broadcasted_iota(jnp.int32, (L, L), 0)
    si = lax.broadcasted_iota(jnp.int32, (L, L), 1)
    tri = si <= ti

    def front(bi):
        ext_scr[bi, 0, 8:8 + L, :] = qpre_ref[bi].astype(F32)
        ext_scr[bi, 1, 8:8 + L, :] = kpre_ref[bi].astype(F32)

        def conv(idx, w_ref, b_ref):
            y = b_ref[...]
            for j in range(ML_CONV):
                lo = 8 - (ML_CONV - 1) + j
                y = y + ext_scr[bi, idx, lo:lo + L, :] * w_ref[j:j + 1, :]
            return y

        qc = conv(0, cwq_ref, cbq_ref)
        kc = conv(1, cwk_ref, cbk_ref)
        tail_q = ext_scr[bi, 0, L:L + 8, :]
        tail_k = ext_scr[bi, 1, L:L + 8, :]
        ext_scr[bi, 0, 0:8, :] = tail_q
        ext_scr[bi, 1, 0:8, :] = tail_k

        q = _silu(qc)
        k = _silu(kc) * (ML_DK ** -0.5)
        qb = q.astype(BF16)
        vb = v_ref[bi].astype(BF16)

        gc = gc_ref[bi, 0]
        gr = gr_ref[bi, 0]
        i_col = gc[:, 0:1] + b_i
        f_col = _log_sigmoid(gc[:, 1:2] + b_f)
        i_row = gr[0:1, :] + b_i
        f_row = _log_sigmoid(gr[1:2, :] + b_f)

        bcum_col = jnp.sum(jnp.where(tri, f_row, 0.0), axis=1, keepdims=True)
        bcum_row = jnp.sum(jnp.where(ti <= si, f_col, 0.0), axis=0, keepdims=True)
        b_last = jnp.sum(f_row, axis=1, keepdims=True)
        m_prev = m_ref[bi, 0]
        cmat = c_ref[bi, 0]
        nrow = n_ref[bi, 0]

        dec_col = b_last - bcum_col + i_col
        dec_row = b_last - bcum_row + i_row
        m_new = jnp.maximum(b_last + m_prev, jnp.max(dec_row, axis=1, keepdims=True))
        wk_col = jnp.exp(dec_col - m_new)
        decay = jnp.exp(b_last + m_prev - m_new)
        kw = wk_col * k

        qk = lax.dot_general(qb, k.astype(BF16), (((1,), (1,)), ((), ())), preferred_element_type=F32)
        q_c = jnp.dot(qb, cmat.astype(BF16), preferred_element_type=F32)
        kv = lax.dot_general(kw.astype(BF16), vb, (((0,), (0,)), ((), ())), preferred_element_type=F32)
        c_ref[bi, 0] = decay * cmat + kv
        n_ref[bi, 0] = decay * nrow + jnp.sum(kw, axis=0, keepdims=True)
        m_ref[bi, 0] = m_new
        dmat = jnp.where(tri, bcum_col - bcum_row + i_row, NEG)
        inter = bcum_col + m_prev
        q_n = jnp.sum(q * nrow, axis=1, keepdims=True)
        return qk, q_c, q_n, dmat, inter, vb

    def back(bi, vals):
        qk, q_c, q_n, dmat, inter, vb = vals
        mt = jnp.maximum(inter, jnp.max(dmat, axis=1, keepdims=True))
        s = qk * jnp.exp(dmat - mt)
        w_in = jnp.exp(inter - mt)
        num = jnp.dot(s.astype(BF16), vb, preferred_element_type=F32) + w_in * q_c
        den = jnp.sum(s, axis=1, keepdims=True) + w_in * q_n
        hh = num / jnp.maximum(jnp.abs(den), jnp.exp(-mt))

        mu = jnp.mean(hh, axis=1, keepdims=True)
        xc = hh - mu
        var = jnp.mean(xc * xc, axis=1, keepdims=True)
        hn = xc * lax.rsqrt(var + LN_EPS) * g_ref[...]
        o_ref[bi] = (_sigmoid(mo_ref[bi].astype(F32)) * hn).astype(o_ref.dtype)

    nxt = front(0)
    for bi in range(bs):
        cur = nxt
        if bi + 1 < bs:
            nxt = front(bi + 1)
        back(bi, cur)


def _mlstm(p3, gates_col, gates_row, conv8, conv_w, conv_b, norm_g, b_i, b_f, c0, n0, m0, L, bs):
    b, t, _ = p3.shape
    nc = t // L
    assert bs == 1 or nc == 1
    smem = pl.BlockSpec(memory_space=pltpu.SMEM)
    qb0 = COL_MQK // ML_DK
    kb0 = (COL_MQK + ML_QK_W) // ML_DK
    vb0 = COL_MV // ML_DV
    ob0 = COL_MO // ML_DV
    kq0 = ML_QK_W // ML_DK
    return pl.pallas_call(
        _mlstm_kernel,
        grid=(b // bs, ML_H, nc),
        in_specs=[smem, smem,
                  pl.BlockSpec((bs, L, ML_DK), lambda i, h, c: (i, c, qb0 + h)),
                  pl.BlockSpec((bs, L, ML_DK), lambda i, h, c: (i, c, kb0 + h)),
                  pl.BlockSpec((bs, L, ML_DV), lambda i, h, c: (i, c, vb0 + h)),
                  pl.BlockSpec((bs, L, ML_DV), lambda i, h, c: (i, c, ob0 + h)),
                  pl.BlockSpec((bs, 1, L, 2), lambda i, h, c: (i * nc + c, h, 0, 0)),
                  pl.BlockSpec((bs, 1, 2, L), lambda i, h, c: (i * nc + c, h, 0, 0)),
                  pl.BlockSpec((bs, 8, ML_DK), lambda i, h, c: (i, 0, h)),
                  pl.BlockSpec((bs, 8, ML_DK), lambda i, h, c: (i, 0, kq0 + h)),
                  pl.BlockSpec((ML_CONV, ML_DK), lambda i, h, c: (0, h)),
                  pl.BlockSpec((ML_CONV, ML_DK), lambda i, h, c: (0, kq0 + h)),
                  pl.BlockSpec((1, ML_DK), lambda i, h, c: (0, h)),
                  pl.BlockSpec((1, ML_DK), lambda i, h, c: (0, kq0 + h)),
                  pl.BlockSpec((1, ML_DV), lambda i, h, c: (0, h)),
                  pl.BlockSpec((bs, 1, ML_DK, ML_DV), lambda i, h, c: (i, h, 0, 0)),
                  pl.BlockSpec((bs, 1, 1, ML_DK), lambda i, h, c: (i, h, 0, 0)),
                  pl.BlockSpec((bs, 1, 1, 1), lambda i, h, c: (i, h, 0, 0))],
        out_specs=[pl.BlockSpec((bs, L, ML_DV), lambda i, h, c: (i, c, h)),
                   pl.BlockSpec((bs, 1, ML_DK, ML_DV), lambda i, h, c: (i, h, 0, 0)),
                   pl.BlockSpec((bs, 1, 1, ML_DK), lambda i, h, c: (i, h, 0, 0)),
                   pl.BlockSpec((bs, 1, 1, 1), lambda i, h, c: (i, h, 0, 0))],
        out_shape=[jax.ShapeDtypeStruct((b, t, ML_V_W), F32),
                   jax.ShapeDtypeStruct((b, ML_H, ML_DK, ML_DV), F32),
                   jax.ShapeDtypeStruct((b, ML_H, 1, ML_DK), F32),
                   jax.ShapeDtypeStruct((b, ML_H, 1, 1), F32)],
        scratch_shapes=[pltpu.VMEM((bs, 2, L + 8, ML_DK), F32)],
        compiler_params=_cparams(("parallel", "parallel", "arbitrary")),
        name="mlstm",
    )(b_i, b_f, p3, p3, p3, p3, gates_col, gates_row, conv8, conv8, conv_w, conv_w,
      conv_b, conv_b, norm_g, c0, n0, m0)


def _layer_norm(r, g, b):
    mu = jnp.mean(r, axis=-1, keepdims=True)
    xc = r - mu
    var = jnp.mean(xc * xc, axis=-1, keepdims=True)
    return xc * lax.rsqrt(var + LN_EPS) * g + b


def _mix_kernel(att_ref, hm_ref, ga0_ref, ga1_ref, gb0_ref, gb1_ref, x_ref, g1_ref, sc2_ref, sh2_ref,
                wa_ref, wb_ref, wo_ref, lg_ref, lb_ref, x1_ref, h2t_ref):
    ya = jnp.dot(att_ref[...].astype(BF16), wa_ref[...], preferred_element_type=F32)
    yb = jnp.dot(hm_ref[...].astype(BF16), wb_ref[...], preferred_element_type=F32)
    ga = jnp.concatenate([ga0_ref[...], ga1_ref[...]], axis=1).astype(F32)
    gb = jnp.concatenate([gb0_ref[...], gb1_ref[...]], axis=1).astype(F32)
    z = _sigmoid(ga) * ya + _sigmoid(gb) * yb
    mix = jnp.dot(z.astype(BF16), wo_ref[...], preferred_element_type=F32)
    shp = x_ref.shape
    r = ALPHA * x_ref[...] + g1_ref[...] * mix.reshape(shp)
    x1 = _layer_norm(r, lg_ref[...], lb_ref[...])
    x1_ref[...] = x1
    h2 = x1 * (1.0 + sc2_ref[...]) + sh2_ref[...]
    h2t_ref[...] = h2.reshape(mix.shape).T.astype(BF16)


def _mix(att2, hm2, p2, x, g1, sc2, sh2, wa, wb, wo, ln_g, ln_b, bb, tt):
    bx, tx, d = x.shape
    tm = bb * tt
    ntok = bx * tx
    tpb = tx // tt
    xmap = lambda i: (i // tpb if bb == 1 else i, i % tpb if bb == 1 else 0, 0)
    mmap = lambda i: (i // tpb if bb == 1 else i, 0, 0)
    hw = d // 2
    wspec = lambda shp: pl.BlockSpec(shp, lambda i: (0, 0), pipeline_mode=pl.Buffered(1))
    return pl.pallas_call(
        _mix_kernel,
        grid=(ntok // tm,),
        in_specs=[pl.BlockSpec((tm, d), lambda i: (i, 0)),
                  pl.BlockSpec((tm, d), lambda i: (i, 0)),
                  pl.BlockSpec((tm, hw), lambda i: (i, COL_GA // hw)),
                  pl.BlockSpec((tm, hw), lambda i: (i, COL_GA // hw + 1)),
                  pl.BlockSpec((tm, hw), lambda i: (i, COL_GB // hw)),
                  pl.BlockSpec((tm, hw), lambda i: (i, COL_GB // hw + 1)),
                  pl.BlockSpec((bb, tt, d), xmap),
                  pl.BlockSpec((bb, 1, d), mmap),
                  pl.BlockSpec((bb, 1, d), mmap),
                  pl.BlockSpec((bb, 1, d), mmap),
                  wspec(wa.shape), wspec(wb.shape), wspec(wo.shape),
                  pl.BlockSpec((1, d), lambda i: (0, 0)),
                  pl.BlockSpec((1, d), lambda i: (0, 0))],
        out_specs=[pl.BlockSpec((bb, tt, d), xmap),
                   pl.BlockSpec((d, tm), lambda i: (0, i))],
        out_shape=[jax.ShapeDtypeStruct((bx, tx, d), F32),
                   jax.ShapeDtypeStruct((d, ntok), BF16)],
        compiler_params=_cparams(("parallel",)),
        name="mix",
    )(att2, hm2, p2, p2, p2, p2, x, g1, sc2, sh2, wa, wb, wo, ln_g, ln_b)


def _top16(val, kio):
    work = val
    rank = jnp.full(val.shape, float(TOPK), F32)
    tops = []
    nkeys = val.shape[0]
    for r in range(TOPK):
        m = jnp.max(work, axis=0, keepdims=True)
        idx = jnp.min(jnp.where(work == m, kio, float(nkeys)), axis=0, keepdims=True)
        hit = kio == idx
        rank = jnp.where(hit, float(r), rank)
        work = jnp.where(hit, -jnp.inf, work)
        tops.append(m)
    return rank, tops


def _top16_distinct(val, want_rank):
    work = val
    rank = jnp.full(val.shape, float(TOPK), F32) if want_rank else None
    tops = []
    for r in range(TOPK):
        m = jnp.max(work, axis=0, keepdims=True)
        hit = work == m
        if want_rank:
            rank = jnp.where(hit, float(r), rank)
        work = jnp.where(hit, -jnp.inf, work)
        tops.append(m)
    removed = jnp.sum(jnp.where(work == -jnp.inf, 1.0, 0.0), axis=0, keepdims=True)
    return rank, tops, removed


def _peer_select_kernel(h2t_ref, wq_ref, keys_ref, rb_ref, eb_ref, cnt_ref, ea_ref):
    tl = h2t_ref.shape[1]
    w = min(tl, 128)
    kio = lax.broadcasted_iota(jnp.int32, (NK, w), 0).astype(F32)
    rio = lax.broadcasted_iota(jnp.int32, (TOPK, w), 0).astype(F32)

    def scores(hd):
        qp = jnp.dot(wq_ref[hd * 2 * NK:(hd + 1) * 2 * NK, :], h2t_ref[...],
                     preferred_element_type=F32).astype(BF16)
        return (jnp.dot(keys_ref[2 * hd], qp[:NK], preferred_element_type=F32),
                jnp.dot(keys_ref[2 * hd + 1], qp[NK:], preferred_element_type=F32))

    def head_lanes(hd, ls, a, b, exact):
        if exact:
            ra, atop = _top16(a, kio)
            rb, btop = _top16(b, kio)
            tied = None
        else:
            _, atop, na = _top16_distinct(a, False)
            rb, btop, nb = _top16_distinct(b, True)
            tied = jnp.max(jnp.maximum(na, nb)) > float(TOPK)
        asort = jnp.concatenate(atop, axis=0)
        cnt = jnp.zeros((TOPK, w), F32)
        front = asort + btop[0]
        for _ in range(TOPK):
            m = jnp.max(front, axis=0, keepdims=True)
            idx = jnp.min(jnp.where(front == m, rio, float(TOPK)), axis=0, keepdims=True)
            hit = rio == idx
            cnt = cnt + jnp.where(hit, 1.0, 0.0)
            hr = TOPK // 2
            nxt = jnp.full((hr, w), -jnp.inf, F32)
            for cc in range(1, TOPK):
                nxt = jnp.where(cnt[:hr] == float(cc), btop[cc], nxt)
            nxt = jnp.concatenate([asort[:hr] + nxt, jnp.full((TOPK - hr, w), -jnp.inf, F32)], axis=0)
            front = jnp.where(hit, nxt, front)
        ea_s = jnp.exp(asort - atop[0])
        pref = jnp.zeros((1, w), F32)
        pbsel = jnp.zeros((TOPK, w), F32)
        for cc in range(1, TOPK + 1):
            pref = pref + jnp.exp(btop[cc - 1] - btop[0])
            pbsel = jnp.where(cnt == float(cc), pref, pbsel)
        z = jnp.sum(ea_s * pbsel, axis=0, keepdims=True)
        cnti = jnp.zeros((NK, w), F32)
        for r in range(TOPK):
            match = (ra == float(r)) if exact else (a == atop[r])
            cnti = jnp.where(match, cnt[r:r + 1, :], cnti)
        rb_ref[hd, :, ls] = rb.astype(rb_ref.dtype)
        eb_ref[hd, :, ls] = jnp.exp(b - btop[0]).astype(eb_ref.dtype)
        cnt_ref[hd, :, ls] = cnti
        ea_ref[hd, :, ls] = jnp.exp(a - atop[0]) * (0.5 / z)
        return tied

    def head(hd, a, b, exact):
        tied = None
        for c0 in range(0, tl, w):
            ls = slice(c0, c0 + w)
            t = head_lanes(hd, ls, a[:, ls], b[:, ls], exact)
            if t is not None:
                tied = t if tied is None else jnp.logical_or(tied, t)
        return tied

    tied = []
    nxt = scores(0)
    for hd in range(PEER_H):
        a, b = nxt
        if hd + 1 < PEER_H:
            nxt = scores(hd + 1)
        tied.append(head(hd, a, b, False))

    for hd in range(PEER_H):
        @pl.when(tied[hd])
        def _(hd=hd):
            head(hd, *scores(hd), True)


def _peer_select(h2t, wqt, keys, tl):
    d, ntok = h2t.shape
    out = jax.ShapeDtypeStruct((PEER_H, NK, ntok), F32)
    outb = jax.ShapeDtypeStruct((PEER_H, NK, ntok), jnp.bfloat16)
    ospec = pl.BlockSpec((PEER_H, NK, tl), lambda i: (0, 0, i))
    return pl.pallas_call(
        _peer_select_kernel,
        grid=(ntok // tl,),
        in_specs=[pl.BlockSpec((d, tl), lambda i: (0, i)),
                  pl.BlockSpec(wqt.shape, lambda i: (0, 0), pipeline_mode=pl.Buffered(1)),
                  pl.BlockSpec(keys.shape, lambda i: (0, 0, 0))],
        out_specs=[ospec, ospec, ospec, ospec],
        out_shape=[outb, outb, out, out],
        compiler_params=_cparams(("parallel",)),
        name="peer_select",
    )(h2t, wqt, keys)


def _peer_dense_kernel(h2t_ref, u_ref, vt_ref, rb_ref, eb_ref, cnt_ref, ea_ref,
                       x1_ref, g2_ref, lg_ref, lb_ref, y_ref,
                       o_ref, act0_ref, act1_ref, mt_ref, *, n_e):
    s = pl.program_id(0)
    e_prev = jnp.maximum(s - 1, 0) % n_e
    half = u_ref.shape[0] // 2
    zero = jnp.zeros((), rb_ref.dtype)

    @pl.when(s == 0)
    def _():
        act1_ref[...] = jnp.zeros_like(act1_ref)

    @pl.when(e_prev == 0)
    def _():
        o_ref[...] = jnp.zeros_like(o_ref)

    tm = h2t_ref.shape[1]
    pk = rb_ref.shape[2]

    def row(ref, hd, il):
        return jnp.broadcast_to(ref[hd, il:il + 1, :], (pk, tm)).astype(rb_ref.dtype)[None]

    def gated(old_ref, hf):
        for il in range(hf * half // NK, (hf + 1) * half // NK):
            gate = None
            for hd in range(PEER_H):
                term = jnp.where(rb_ref[hd] < row(cnt_ref, hd, il), eb_ref[hd] * row(ea_ref, hd, il), zero)
                gate = term if gate is None else gate + term
            rows = slice(il * NK, (il + 1) * NK)
            a = old_ref[rows, :]
            half_gelu2 = a * (1.0 + lax.erf(a * (2.0 ** -0.5)))
            mt_ref[rows, :] = (gate.reshape(NK, tm) * half_gelu2.astype(gate.dtype)).astype(mt_ref.dtype)

    def phases(new_ref, old_ref):
        new_ref[...] = jnp.dot(u_ref[...], h2t_ref[...], preferred_element_type=F32)
        for hf in range(2):
            rows = slice(hf * half, (hf + 1) * half)
            gated(old_ref, hf)
            o_ref[...] += jnp.dot(vt_ref[:, rows], mt_ref[rows, :], preferred_element_type=F32)

    @pl.when(s % 2 == 0)
    def _():
        phases(act0_ref, act1_ref)

    @pl.when(s % 2 == 1)
    def _():
        phases(act1_ref, act0_ref)

    @pl.when((s > 0) & (e_prev == n_e - 1))
    def _():
        shp = x1_ref.shape
        r = ALPHA * x1_ref[...] + g2_ref[...] * o_ref[...].T.reshape(shp)
        y_ref[...] = _layer_norm(r, lg_ref[...], lb_ref[...])


def _peer_dense(h2t, u, vt, rb, eb, cnti, ea, x1, g2, ln_g, ln_b, bb, tt):
    d, ntok = h2t.shape
    bx, tx, _ = x1.shape
    tm = bb * tt
    tpb = tx // tt
    et = 8 * NK
    n_e = u.shape[0] // et
    n_items = (ntok // tm) * n_e
    item_a = lambda s: jnp.minimum(s, n_items - 1)
    item_b = lambda s: jnp.maximum(s - 1, 0)
    pk = 16
    rb = rb.reshape(PEER_H, NK // pk, pk, ntok)
    eb = eb.reshape(PEER_H, NK // pk, pk, ntok)
    rspec = pl.BlockSpec((PEER_H, NK // pk, pk, tm), lambda s: (0, 0, 0, item_b(s) // n_e))
    cspec = pl.BlockSpec((PEER_H, 8, tm), lambda s: (0, item_b(s) % n_e, item_b(s) // n_e))
    tile_b = lambda s: item_b(s) // n_e
    xmap = lambda s: (tile_b(s) // tpb if bb == 1 else tile_b(s), tile_b(s) % tpb if bb == 1 else 0, 0)
    mmap = lambda s: (tile_b(s) // tpb if bb == 1 else tile_b(s), 0, 0)
    return pl.pallas_call(
        functools.partial(_peer_dense_kernel, n_e=n_e),
        grid=(n_items + 1,),
        in_specs=[pl.BlockSpec((d, tm), lambda s: (0, item_a(s) // n_e)),
                  pl.BlockSpec((et, d), lambda s: (item_a(s) % n_e, 0)),
                  pl.BlockSpec((d, et), lambda s: (0, item_b(s) % n_e)),
                  rspec, rspec, cspec, cspec,
                  pl.BlockSpec((bb, tt, d), xmap),
                  pl.BlockSpec((bb, 1, d), mmap),
                  pl.BlockSpec((1, d), lambda s: (0, 0)),
                  pl.BlockSpec((1, d), lambda s: (0, 0))],
        out_specs=pl.BlockSpec((bb, tt, d), xmap),
        out_shape=jax.ShapeDtypeStruct((bx, tx, d), F32),
        scratch_shapes=[pltpu.VMEM((d, tm), F32), pltpu.VMEM((et, tm), F32),
                        pltpu.VMEM((et, tm), F32), pltpu.VMEM((et, tm), BF16)],
        compiler_params=_cparams(("arbitrary",)),
        name="peer_dense",
    )(h2t, u, vt, rb, eb, cnti, ea, x1, g2, ln_g, ln_b)


def _tile(bx, tx, want):
    if tx >= want:
        return 1, want
    return want // tx, tx


def _layer(x, ada, state, prm):
    bx, tx, d = x.shape
    ntok = bx * tx
    sh1, sc1, g1, sh2, sc2, g2 = [a[:, None, :] for a in jnp.split(ada, 6, axis=-1)]

    bb, tt = _tile(bx, tx, min(1024, ntok))
    p2, gates2 = _inproj(x, sc1, sh1, prm["w_in1"], prm["w_in2"], prm["w_if"], bb, tt)
    p3 = p2.reshape(bx, tx, IN_MAIN)

    if state is None:
        att = _attn_prompt(p3, prm["bias_prompt"], prm["sink_prompt"])
        keep = min(WINDOW, tx)
        k_keep = p3[:, tx - keep:, COL_AK:COL_AK + ATT_KV_W].astype(F32)
        v_keep = p3[:, tx - keep:, COL_AV:COL_AV + ATT_KV_W].astype(F32)
        conv_buf = jnp.zeros((bx, ML_CONV - 1, 2 * ML_QK_W), F32)
        c0 = jnp.zeros((bx, ML_H, ML_DK, ML_DV), F32)
        n0 = jnp.zeros((bx, ML_H, ML_DK), F32)
        m0 = jnp.zeros((bx, ML_H), F32)
    else:
        kwin, vwin, conv_buf, c0, n0, m0 = state
        wc = kwin.shape[1]
        att, k_keep, v_keep = _attn_sample(
            p3, kwin.reshape(bx, wc, ATT_KV_W), vwin.reshape(bx, wc, ATT_KV_W),
            prm["bias_sample_w"], prm["bias_sample_n"], prm["sink_sample"], math.gcd(bx, 8))
    k_keep = k_keep.reshape(bx, -1, ATT_KV, HD)
    v_keep = v_keep.reshape(bx, -1, ATT_KV, HD)

    L = min(ML_CHUNK_MAX, tx)
    nc = tx // L
    gates = gates2[:, :2 * ML_H].reshape(bx * nc, L, 2, ML_H)
    gates_col = jnp.transpose(gates, (0, 3, 1, 2))
    gates_row = jnp.transpose(gates, (0, 3, 2, 1))
    conv8 = jnp.pad(conv_buf.astype(F32), ((0, 0), (8 - (ML_CONV - 1), 0), (0, 0)))
    hm, c1, n1, m1 = _mlstm(p3, gates_col, gates_row, conv8, prm["conv_w"], prm["conv_b"],
                            prm["norm_g"], prm["b_i"], prm["b_f"], c0.astype(F32),
                            n0.astype(F32).reshape(bx, ML_H, 1, ML_DK),
                            m0.astype(F32).reshape(bx, ML_H, 1, 1), L,
                            math.gcd(bx, 8) if nc == 1 else 1)
    conv_keep = p3[:, tx - (ML_CONV - 1):, COL_MQK:COL_MQK + 2 * ML_QK_W].astype(F32)
    new_state = (k_keep, v_keep, conv_keep, c1, n1.reshape(bx, ML_H, ML_DK), m1.reshape(bx, ML_H))

    bb, tt = _tile(bx, tx, min(256, ntok))
    x1, h2t = _mix(att.reshape(ntok, ATT_Q_W), hm.reshape(ntok, ML_V_W), p2, x, g1, sc2, sh2,
                   prm["w_a"], prm["w_b"], prm["w_out"], prm["ln1_g"], prm["ln1_b"], bb, tt)

    rb, eb, cnti, ea = _peer_select(h2t, prm["w_pqt"], prm["keys"], min(256, ntok))
    bb, tt = _tile(bx, tx, min(512, ntok))
    y = _peer_dense(h2t, prm["peer_u"], prm["peer_vt"], rb, eb, cnti, ea,
                    x1, g2, prm["ln2_g"], prm["ln2_b"], bb, tt)
    return y, new_state


def kernel(x_prompt, x_sample, c_prompt, c_sample, cache_k_win, cache_v_win, state_conv, state_C, state_n, state_m, rel_bias_table, w_ada, b_ada, w_in, b_i, b_f, conv_w, conv_b, ml_norm_g, att_sinks, w_a, w_b, w_out, ln1_g, ln1_b, w_pq, peer_sub_keys, peer_u, peer_v, ln2_g, ln2_b):
    assert w_ada.shape[0] == DEPTH
    d = x_prompt.shape[-1]
    bp = x_prompt.shape[0]
    ts = x_sample.shape[1]
    wc = cache_k_win.shape[2]

    wi = w_in[0]
    n_if = 2 * ML_H
    assert wi.shape[1] == IN_MAIN + n_if
    w_in1 = wi.astype(BF16)
    w_in2 = w_in1[:, IN_W1 + n_if:]
    w_if = jnp.pad(wi[:, IN_W1:IN_W1 + n_if], ((0, 0), (0, IF_PAD - n_if))).astype(BF16)

    sinks = att_sinks[0].astype(F32).reshape(ATT_KV, ATT_G, 1, 1)
    blk = WINDOW
    dist_p = (jnp.arange(blk)[:, None] + blk) - jnp.arange(2 * blk)[None, :]
    dist_s = (jnp.arange(ts)[:, None] + wc) - jnp.arange(wc + ts)[None, :]
    bias_s = _bias_table(rel_bias_table, dist_s)
    bias_p = _bias_table_t(rel_bias_table, dist_p)
    prm = {
        "w_in1": w_in1, "w_in2": w_in2, "w_if": w_if,
        "bias_prompt": jnp.stack([jnp.where(jnp.arange(2 * blk)[None, :, None] < blk, NEG, bias_p), bias_p]),
        "sink_prompt": jnp.broadcast_to(att_sinks[0].astype(F32)[:, None, None], (ATT_HEADS, 1, blk)),
        "bias_sample_w": bias_s[:, :, :wc],
        "bias_sample_n": bias_s[:, :, wc:],
        "sink_sample": jnp.broadcast_to(sinks, (ATT_KV, ATT_G, ts, 1)).reshape(ATT_KV, ATT_G * ts, 1),
        "conv_w": conv_w[0], "conv_b": conv_b[0].reshape(1, -1), "norm_g": ml_norm_g[0].reshape(1, -1),
        "b_i": b_i[0], "b_f": b_f[0],
        "w_a": w_a[0].astype(BF16), "w_b": w_b[0].astype(BF16), "w_out": w_out[0].astype(BF16),
        "ln1_g": ln1_g[0].reshape(1, -1), "ln1_b": ln1_b[0].reshape(1, -1),
        "w_pqt": w_pq[0].T.astype(BF16),
        "keys": peer_sub_keys[0].reshape(2 * PEER_H, NK, -1).astype(BF16),
        "peer_u": peer_u[0].astype(BF16),
        "peer_vt": peer_v[0].T.astype(BF16),
        "ln2_g": ln2_g[0].reshape(1, -1), "ln2_b": ln2_b[0].reshape(1, -1),
    }

    ada = _ada(jnp.concatenate([c_prompt, c_sample], axis=0), w_ada[0], b_ada[0])
    yp, sp = _layer(x_prompt, ada[:bp], None, prm)
    ys, ss = _layer(x_sample, ada[bp:],
                    (cache_k_win[0], cache_v_win[0], state_conv[0], state_C[0], state_n[0], state_m[0]), prm)
    stack = lambda s: [a[None] for a in s]
    return (yp, ys, *stack(sp), *stack(ss))
```

```python
import functools
import math

import jax
import jax.numpy as jnp
from jax import lax
from jax.experimental import pallas as pl
from jax.experimental.pallas import tpu as pltpu

F32 = jnp.float32
BF16 = jnp.bfloat16

ATT_HEADS = 32
ATT_KV = 8
ATT_G = ATT_HEADS // ATT_KV
HD = 64
WINDOW = 128
N_BUCKETS = 32
MAX_DISTANCE = 128
ML_H = 4
ML_DK = 256
ML_DV = 512
ML_CONV = 4
PEER_H = 8
NK = 128
TOPK = 16
ATT_Q_W = ATT_HEADS * HD
ATT_KV_W = ATT_KV * HD
ML_QK_W = ML_H * ML_DK
ML_V_W = ML_H * ML_DV
DEPTH = 1
ALPHA = (2.0 * DEPTH) ** 0.25
LN_EPS = 1e-5
NEG = -1e30

COL_AQ = 0
COL_AK = 2048
COL_AV = 2560
COL_MQK = 3072
COL_MV = 5120
IN_W1 = 7168
COL_MO = 7168
COL_GA = 9216
COL_GB = 11264
IN_MAIN = 13312
IN_TN = 1024
IF_PAD = 128

ML_CHUNK_MAX = 256
VMEM_LIMIT = 56 * 1024 * 1024


def _cparams(sem):
    return pltpu.CompilerParams(dimension_semantics=sem, vmem_limit_bytes=VMEM_LIMIT)


def _sigmoid(x):
    return 1.0 / (1.0 + jnp.exp(-x))


def _silu(x):
    return x * _sigmoid(x)


def _log_sigmoid(x):
    return jnp.minimum(x, 0.0) - jnp.log1p(jnp.exp(-jnp.abs(x)))


def _ada_kernel(c_ref, w_ref, b_ref, o_ref):
    c = c_ref[...]
    a = _silu(c).astype(BF16)
    o_ref[...] = jnp.dot(a, w_ref[...].astype(BF16), preferred_element_type=F32) + b_ref[...]


def _ada(c_all, w_ada, b_ada):
    r, d = c_all.shape
    n = w_ada.shape[1]
    tn = 1024
    return pl.pallas_call(
        _ada_kernel,
        grid=(n // tn,),
        in_specs=[pl.BlockSpec((r, d), lambda j: (0, 0)),
                  pl.BlockSpec((d, tn), lambda j: (0, j)),
                  pl.BlockSpec((1, tn), lambda j: (0, j))],
        out_specs=pl.BlockSpec((r, tn), lambda j: (0, j)),
        out_shape=jax.ShapeDtypeStruct((r, n), F32),
        compiler_params=_cparams(("parallel",)),
        name="ada",
    )(c_all, w_ada, b_ada.reshape(1, n))


def _inproj_kernel(x_ref, sc_ref, sh_ref, w1_ref, w2_ref, wif_ref, p_ref, if_ref, h_scr, *, n1):
    j = pl.program_id(1)

    @pl.when(j == 0)
    def _():
        h = x_ref[...] * (1.0 + sc_ref[...]) + sh_ref[...]
        hb = h.reshape(h_scr.shape).astype(BF16)
        h_scr[...] = hb
        if_ref[...] = jnp.dot(hb, wif_ref[...], preferred_element_type=F32)

    @pl.when(j < n1)
    def _():
        p_ref[...] = jnp.dot(h_scr[...], w1_ref[...], preferred_element_type=F32).astype(p_ref.dtype)

    @pl.when(j >= n1)
    def _():
        p_ref[...] = jnp.dot(h_scr[...], w2_ref[...], preferred_element_type=F32).astype(p_ref.dtype)


def _inproj(x, sc, sh, w1, w2, wif, bb, tt):
    bx, tx, d = x.shape
    tm = bb * tt
    ntok = bx * tx
    n1 = IN_W1 // IN_TN
    n2 = w2.shape[1] // IN_TN
    tpb = tx // tt
    xmap = lambda i, j: (i // tpb if bb == 1 else i, i % tpb if bb == 1 else 0, 0)
    mmap = lambda i, j: (i // tpb if bb == 1 else i, 0, 0)
    return pl.pallas_call(
        functools.partial(_inproj_kernel, n1=n1),
        grid=(ntok // tm, n1 + n2),
        in_specs=[pl.BlockSpec((bb, tt, d), xmap),
                  pl.BlockSpec((bb, 1, d), mmap),
                  pl.BlockSpec((bb, 1, d), mmap),
                  pl.BlockSpec((d, IN_TN), lambda i, j: (0, jnp.minimum(j, n1 - 1))),
                  pl.BlockSpec((d, IN_TN), lambda i, j: (0, jnp.maximum(j - n1, 0))),
                  pl.BlockSpec((d, IF_PAD), lambda i, j: (0, 0))],
        out_specs=[pl.BlockSpec((tm, IN_TN), lambda i, j: (i, j)),
                   pl.BlockSpec((tm, IF_PAD), lambda i, j: (i, 0))],
        out_shape=[jax.ShapeDtypeStruct((ntok, IN_MAIN), BF16),
                   jax.ShapeDtypeStruct((ntok, IF_PAD), F32)],
        scratch_shapes=[pltpu.VMEM((tm, d), BF16)],
        compiler_params=_cparams(("parallel", "arbitrary")),
        name="inproj",
    )(x, sc, sh, w1, w2, wif)


def _t5_bucket(dist):
    max_exact = N_BUCKETS // 2
    d = jnp.maximum(dist, 0)
    dl = jnp.maximum(d, max_exact).astype(F32)
    large = max_exact + (jnp.log(dl / max_exact) / math.log(MAX_DISTANCE / max_exact)
                         * (N_BUCKETS - max_exact)).astype(jnp.int32)
    large = jnp.minimum(large, N_BUCKETS - 1)
    return jnp.where(d < max_exact, d, large)


def _bias_table(rel_table, dist):
    lq, lk = dist.shape
    band = (dist >= 0) & (dist < WINDOW)
    bias = jnp.transpose(rel_table[_t5_bucket(dist)], (2, 0, 1)).astype(F32)
    bias = jnp.where(band[None], bias, NEG)
    return bias.reshape(ATT_KV, ATT_G * lq, lk)


def _bias_table_t(rel_table, dist):
    lq, lk = dist.shape
    bucket = _t5_bucket(dist).T.reshape(1, lk * lq)
    onehot = (bucket == jnp.arange(N_BUCKETS)[:, None]).astype(F32)
    bias = jnp.dot(rel_table.T.astype(F32), onehot, precision=lax.Precision.HIGHEST)
    band = ((dist >= 0) & (dist < WINDOW)).T.reshape(1, lk * lq)
    return jnp.where(band, bias, NEG).reshape(-1, lk, lq)


def _attn_prompt_kernel(q_ref, kp_ref, kc_ref, vp_ref, vc_ref, bias_ref, sink_ref, o_ref):
    qt = (q_ref[0] * (HD ** -0.5)).T.astype(BF16)
    kcat = jnp.concatenate([kp_ref[0], kc_ref[0]], axis=0).astype(BF16)
    vt = jnp.concatenate([vp_ref[0], vc_ref[0]], axis=0).T.astype(BF16)

    def scores(kv):
        kh = kcat[:, kv * HD:(kv + 1) * HD]
        return [jnp.dot(kh, qt[h * HD:(h + 1) * HD, :], preferred_element_type=F32) + bias_ref[0, h]
                for h in range(kv * ATT_G, (kv + 1) * ATT_G)]

    def finish(kv, sts):
        vth = vt[kv * HD:(kv + 1) * HD, :]
        ps, dens = [], []
        for g, st in enumerate(sts):
            sink = sink_ref[kv * ATT_G + g]
            m = jnp.maximum(jnp.max(st, axis=0, keepdims=True), sink)
            p = jnp.exp(st - m)
            dens.append(jnp.sum(p, axis=0, keepdims=True) + jnp.exp(sink - m))
            ps.append(p.astype(BF16))
        outs = [jnp.dot(vth, p, preferred_element_type=F32) / den for p, den in zip(ps, dens)]
        for g in range(0, ATT_G, 2):
            h = kv * ATT_G + g
            o_ref[0, :, h * HD:(h + 2) * HD] = jnp.concatenate(outs[g:g + 2], axis=0).T.astype(o_ref.dtype)

    nxt = scores(0)
    for kv in range(ATT_KV):
        cur = nxt
        if kv + 1 < ATT_KV:
            nxt = scores(kv + 1)
        finish(kv, cur)


def _attn_prompt(p3, bias2, sinkrow):
    b, t, _ = p3.shape
    blk = WINDOW
    nb = t // blk
    kcol, vcol = COL_AK // ATT_KV_W, COL_AV // ATT_KV_W
    return pl.pallas_call(
        _attn_prompt_kernel,
        grid=(b, nb),
        in_specs=[pl.BlockSpec((1, blk, ATT_Q_W), lambda i, n: (i, n, 0)),
                  pl.BlockSpec((1, blk, ATT_KV_W), lambda i, n: (i, jnp.maximum(n - 1, 0), kcol)),
                  pl.BlockSpec((1, blk, ATT_KV_W), lambda i, n: (i, n, kcol)),
                  pl.BlockSpec((1, blk, ATT_KV_W), lambda i, n: (i, jnp.maximum(n - 1, 0), vcol)),
                  pl.BlockSpec((1, blk, ATT_KV_W), lambda i, n: (i, n, vcol)),
                  pl.BlockSpec((1,) + bias2.shape[1:], lambda i, n: (jnp.minimum(n, 1), 0, 0, 0)),
                  pl.BlockSpec(sinkrow.shape, lambda i, n: (0, 0, 0))],
        out_specs=pl.BlockSpec((1, blk, ATT_Q_W), lambda i, n: (i, n, 0)),
        out_shape=jax.ShapeDtypeStruct((b, t, ATT_Q_W), BF16),
        compiler_params=_cparams(("parallel", "arbitrary")),
        name="attn_prompt",
    )(p3, p3, p3, p3, p3, bias2, sinkrow)


def _attn_sample_kernel(q_ref, kn_ref, vn_ref, kw_ref, vw_ref, bw_ref, bn_ref, sink_ref,
                        o_ref, ko_ref, vo_ref):
    bs, t = q_ref.shape[0], q_ref.shape[1]
    wc = kw_ref.shape[1]
    nt = (((1,), (1,)), ((), ()))

    rows = ATT_G * t

    def scores(si):
        q = q_ref[si].astype(F32) * (HD ** -0.5)
        blocks = []
        for kv in range(ATT_KV):
            qs = jnp.concatenate(
                [q[:, (kv * ATT_G + g) * HD:(kv * ATT_G + g + 1) * HD] for g in range(ATT_G)], axis=0)
            parts = []
            if kv > 0:
                parts.append(jnp.zeros((rows, kv * HD), F32))
            parts.append(qs)
            if kv + 1 < ATT_KV:
                parts.append(jnp.zeros((rows, (ATT_KV - 1 - kv) * HD), F32))
            blocks.append(jnp.concatenate(parts, axis=1))
        qbd = jnp.concatenate(blocks, axis=0).astype(BF16)
        sw = lax.dot_general(qbd, kw_ref[si].astype(BF16), nt, preferred_element_type=F32)
        sn = lax.dot_general(qbd, kn_ref[si].astype(BF16), nt, preferred_element_type=F32)
        return sw + bw_ref[...], sn + bn_ref[...]

    def finish(si, sts):
        kn = kn_ref[si]
        vn = vn_ref[si]
        kw = kw_ref[si]
        vw = vw_ref[si]
        ko_ref[si, 0:wc - t, :] = kw[t:, :]
        ko_ref[si, wc - t:wc, :] = kn.astype(F32)
        vo_ref[si, 0:wc - t, :] = vw[t:, :]
        vo_ref[si, wc - t:wc, :] = vn.astype(F32)
        sw, sn = sts
        sink = sink_ref[...]
        m = jnp.maximum(jnp.maximum(jnp.max(sw, axis=-1, keepdims=True),
                                    jnp.max(sn, axis=-1, keepdims=True)), sink)
        pw = jnp.exp(sw - m)
        pn = jnp.exp(sn - m)
        den = (jnp.sum(pw, axis=-1, keepdims=True) + jnp.sum(pn, axis=-1, keepdims=True)
               + jnp.exp(sink - m))
        o = (jnp.dot(pw.astype(BF16), vw.astype(BF16), preferred_element_type=F32)
             + jnp.dot(pn.astype(BF16), vn.astype(BF16), preferred_element_type=F32)) / den
        outs = []
        for kv in range(ATT_KV):
            for g in range(ATT_G):
                r0 = kv * rows + g * t
                outs.append(o[r0:r0 + t, kv * HD:(kv + 1) * HD])
        o_ref[si] = jnp.concatenate(outs, axis=-1).astype(o_ref.dtype)

    nxt = scores(0)
    for si in range(bs):
        cur = nxt
        if si + 1 < bs:
            nxt = scores(si + 1)
        finish(si, cur)


def _attn_sample(p3, kwin, vwin, bias_w, bias_n, sinkcol, bs):
    b, t, _ = p3.shape
    wc = kwin.shape[1]
    kcol, vcol = COL_AK // ATT_KV_W, COL_AV // ATT_KV_W
    seq3 = lambda i: (i, 0, 0)
    const3 = lambda i: (0, 0)
    return pl.pallas_call(
        _attn_sample_kernel,
        grid=(b // bs,),
        in_specs=[pl.BlockSpec((bs, t, ATT_Q_W), seq3),
                  pl.BlockSpec((bs, t, ATT_KV_W), lambda i: (i, 0, kcol)),
                  pl.BlockSpec((bs, t, ATT_KV_W), lambda i: (i, 0, vcol)),
                  pl.BlockSpec((bs, wc, ATT_KV_W), seq3),
                  pl.BlockSpec((bs, wc, ATT_KV_W), seq3),
                  pl.BlockSpec(bias_w.shape, const3),
                  pl.BlockSpec(bias_n.shape, const3),
                  pl.BlockSpec(sinkcol.shape, const3)],
        out_specs=[pl.BlockSpec((bs, t, ATT_Q_W), seq3),
                   pl.BlockSpec((bs, wc, ATT_KV_W), seq3),
                   pl.BlockSpec((bs, wc, ATT_KV_W), seq3)],
        out_shape=[jax.ShapeDtypeStruct((b, t, ATT_Q_W), F32),
                   jax.ShapeDtypeStruct((b, wc, ATT_KV_W), F32),
                   jax.ShapeDtypeStruct((b, wc, ATT_KV_W), F32)],
        compiler_params=_cparams(("parallel",)),
        name="attn_sample",
    )(p3, p3, p3, kwin, vwin, bias_w, bias_n, sinkcol)


def _mlstm_kernel(bi_ref, bf_ref, qpre_ref, kpre_ref, v_ref, mo_ref, gc_ref, gr_ref,
                  cq_ref, ck_ref, cwq_ref, cwk_ref, cbq_ref, cbk_ref, g_ref,
                  c0_ref, n0_ref, m0_ref,
                  o_ref, c_ref, n_ref, m_ref, ext_scr):
    h = pl.program_id(1)
    c = pl.program_id(2)
    bs, L = qpre_ref.shape[0], qpre_ref.shape[1]

    @pl.when(c == 0)
    def _():
        c_ref[...] = c0_ref[...]
        n_ref[...] = n0_ref[...]
        m_ref[...] = m0_ref[...]
        ext_scr[:, 0, 0:8, :] = cq_ref[...]
        ext_scr[:, 1, 0:8, :] = ck_ref[...]

    b_i = bi_ref[h]
    b_f = bf_ref[h]
    ti = lax.broadcasted_iota(jnp.int32, (L, L), 0)
    si = lax.broadcasted_iota(jnp.int32, (L, L), 1)
    tri = si <= ti

    def front(bi):
        ext_scr[bi, 0, 8:8 + L, :] = qpre_ref[bi].astype(F32)
        ext_scr[bi, 1, 8:8 + L, :] = kpre_ref[bi].astype(F32)

        def conv(idx, w_ref, b_ref):
            y = b_ref[...]
            for j in range(ML_CONV):
                lo = 8 - (ML_CONV - 1) + j
                y = y + ext_scr[bi, idx, lo:lo + L, :] * w_ref[j:j + 1, :]
            return y

        qc = conv(0, cwq_ref, cbq_ref)
        kc = conv(1, cwk_ref, cbk_ref)
        tail_q = ext_scr[bi, 0, L:L + 8, :]
        tail_k = ext_scr[bi, 1, L:L + 8, :]
        ext_scr[bi, 0, 0:8, :] = tail_q
        ext_scr[bi, 1, 0:8, :] = tail_k

        q = _silu(qc)
        k = _silu(kc) * (ML_DK ** -0.5)
        qb = q.astype(BF16)
        vb = v_ref[bi].astype(BF16)

        gc = gc_ref[bi, 0]
        gr = gr_ref[bi, 0]
        i_col = gc[:, 0:1] + b_i
        f_col = _log_sigmoid(gc[:, 1:2] + b_f)
        i_row = gr[0:1, :] + b_i
        f_row = _log_sigmoid(gr[1:2, :] + b_f)

        bcum_col = jnp.sum(jnp.where(tri, f_row, 0.0), axis=1, keepdims=True)
        bcum_row = jnp.sum(jnp.where(ti <= si, f_col, 0.0), axis=0, keepdims=True)
        b_last = jnp.sum(f_row, axis=1, keepdims=True)
        m_prev = m_ref[bi, 0]
        cmat = c_ref[bi, 0]
        nrow = n_ref[bi, 0]

        dec_col = b_last - bcum_col + i_col
        dec_row = b_last - bcum_row + i_row
        m_new = jnp.maximum(b_last + m_prev, jnp.max(dec_row, axis=1, keepdims=True))
        wk_col = jnp.exp(dec_col - m_new)
        decay = jnp.exp(b_last + m_prev - m_new)
        kw = wk_col * k

        qk = lax.dot_general(qb, k.astype(BF16), (((1,), (1,)), ((), ())), preferred_element_type=F32)
        q_c = jnp.dot(qb, cmat.astype(BF16), preferred_element_type=F32)
        kv = lax.dot_general(kw.astype(BF16), vb, (((0,), (0,)), ((), ())), preferred_element_type=F32)
        c_ref[bi, 0] = decay * cmat + kv
        n_ref[bi, 0] = decay * nrow + jnp.sum(kw, axis=0, keepdims=True)
        m_ref[bi, 0] = m_new
        dmat = jnp.where(tri, bcum_col - bcum_row + i_row, NEG)
        inter = bcum_col + m_prev
        q_n = jnp.sum(q * nrow, axis=1, keepdims=True)
        return qk, q_c, q_n, dmat, inter, vb

    def back(bi, vals):
        qk, q_c, q_n, dmat, inter, vb = vals
        mt = jnp.maximum(inter, jnp.max(dmat, axis=1, keepdims=True))
        s = qk * jnp.exp(dmat - mt)
        w_in = jnp.exp(inter - mt)
        num = jnp.dot(s.astype(BF16), vb, preferred_element_type=F32) + w_in * q_c
        den = jnp.sum(s, axis=1, keepdims=True) + w_in * q_n
        hh = num / jnp.maximum(jnp.abs(den), jnp.exp(-mt))

        mu = jnp.mean(hh, axis=1, keepdims=True)
        xc = hh - mu
        var = jnp.mean(xc * xc, axis=1, keepdims=True)
        hn = xc * lax.rsqrt(var + LN_EPS) * g_ref[...]
        o_ref[bi] = (_sigmoid(mo_ref[bi].astype(F32)) * hn).astype(o_ref.dtype)

    nxt = front(0)
    for bi in range(bs):
        cur = nxt
        if bi + 1 < bs:
            nxt = front(bi + 1)
        back(bi, cur)


def _mlstm(p3, gates_col, gates_row, conv8, conv_w, conv_b, norm_g, b_i, b_f, c0, n0, m0, L, bs):
    b, t, _ = p3.shape
    nc = t // L
    assert bs == 1 or nc == 1
    smem = pl.BlockSpec(memory_space=pltpu.SMEM)
    qb0 = COL_MQK // ML_DK
    kb0 = (COL_MQK + ML_QK_W) // ML_DK
    vb0 = COL_MV // ML_DV
    ob0 = COL_MO // ML_DV
    kq0 = ML_QK_W // ML_DK
    return pl.pallas_call(
        _mlstm_kernel,
        grid=(b // bs, ML_H, nc),
        in_specs=[smem, smem,
                  pl.BlockSpec((bs, L, ML_DK), lambda i, h, c: (i, c, qb0 + h)),
                  pl.BlockSpec((bs, L, ML_DK), lambda i, h, c: (i, c, kb0 + h)),
                  pl.BlockSpec((bs, L, ML_DV), lambda i, h, c: (i, c, vb0 + h)),
                  pl.BlockSpec((bs, L, ML_DV), lambda i, h, c: (i, c, ob0 + h)),
                  pl.BlockSpec((bs, 1, L, 2), lambda i, h, c: (i * nc + c, h, 0, 0)),
                  pl.BlockSpec((bs, 1, 2, L), lambda i, h, c: (i * nc + c, h, 0, 0)),
                  pl.BlockSpec((bs, 8, ML_DK), lambda i, h, c: (i, 0, h)),
                  pl.BlockSpec((bs, 8, ML_DK), lambda i, h, c: (i, 0, kq0 + h)),
                  pl.BlockSpec((ML_CONV, ML_DK), lambda i, h, c: (0, h)),
                  pl.BlockSpec((ML_CONV, ML_DK), lambda i, h, c: (0, kq0 + h)),
                  pl.BlockSpec((1, ML_DK), lambda i, h, c: (0, h)),
                  pl.BlockSpec((1, ML_DK), lambda i, h, c: (0, kq0 + h)),
                  pl.BlockSpec((1, ML_DV), lambda i, h, c: (0, h)),
                  pl.BlockSpec((bs, 1, ML_DK, ML_DV), lambda i, h, c: (i, h, 0, 0)),
                  pl.BlockSpec((bs, 1, 1, ML_DK), lambda i, h, c: (i, h, 0, 0)),
                  pl.BlockSpec((bs, 1, 1, 1), lambda i, h, c: (i, h, 0, 0))],
        out_specs=[pl.BlockSpec((bs, L, ML_DV), lambda i, h, c: (i, c, h)),
                   pl.BlockSpec((bs, 1, ML_DK, ML_DV), lambda i, h, c: (i, h, 0, 0)),
                   pl.BlockSpec((bs, 1, 1, ML_DK), lambda i, h, c: (i, h, 0, 0)),
                   pl.BlockSpec((bs, 1, 1, 1), lambda i, h, c: (i, h, 0, 0))],
        out_shape=[jax.ShapeDtypeStruct((b, t, ML_V_W), F32),
                   jax.ShapeDtypeStruct((b, ML_H, ML_DK, ML_DV), F32),
                   jax.ShapeDtypeStruct((b, ML_H, 1, ML_DK), F32),
                   jax.ShapeDtypeStruct((b, ML_H, 1, 1), F32)],
        scratch_shapes=[pltpu.VMEM((bs, 2, L + 8, ML_DK), F32)],
        compiler_params=_cparams(("parallel", "parallel", "arbitrary")),
        name="mlstm",
    )(b_i, b_f, p3, p3, p3, p3, gates_col, gates_row, conv8, conv8, conv_w, conv_w,
      conv_b, conv_b, norm_g, c0, n0, m0)


def _layer_norm(r, g, b):
    mu = jnp.mean(r, axis=-1, keepdims=True)
    xc = r - mu
    var = jnp.mean(xc * xc, axis=-1, keepdims=True)
    return xc * lax.rsqrt(var + LN_EPS) * g + b


def _mix_kernel(att_ref, hm_ref, ga0_ref, ga1_ref, gb0_ref, gb1_ref, x_ref, g1_ref, sc2_ref, sh2_ref,
                wa_ref, wb_ref, wo_ref, lg_ref, lb_ref, x1_ref, h2t_ref):
    att = att_ref[...].astype(BF16)
    hm = hm_ref[...].astype(BF16)
    hw = ga0_ref.shape[1]
    branches = []
    for c in range(2):
        cs = slice(c * hw, (c + 1) * hw)
        branches.append((jnp.dot(att, wa_ref[:, cs], preferred_element_type=F32),
                         jnp.dot(hm, wb_ref[:, cs], preferred_element_type=F32)))
    mix = None
    for c, (ga_ref, gb_ref) in enumerate(((ga0_ref, gb0_ref), (ga1_ref, gb1_ref))):
        ya, yb = branches[c]
        z = _sigmoid(ga_ref[...].astype(F32)) * ya + _sigmoid(gb_ref[...].astype(F32)) * yb
        part = jnp.dot(z.astype(BF16), wo_ref[c * hw:(c + 1) * hw, :], preferred_element_type=F32)
        mix = part if mix is None else mix + part
    shp = x_ref.shape
    r = ALPHA * x_ref[...] + g1_ref[...] * mix.reshape(shp)
    x1 = _layer_norm(r, lg_ref[...], lb_ref[...])
    x1_ref[...] = x1
    h2 = x1 * (1.0 + sc2_ref[...]) + sh2_ref[...]
    h2t_ref[...] = h2.reshape(mix.shape).T.astype(BF16)


def _mix(att2, hm2, p2, x, g1, sc2, sh2, wa, wb, wo, ln_g, ln_b, bb, tt):
    bx, tx, d = x.shape
    tm = bb * tt
    ntok = bx * tx
    tpb = tx // tt
    xmap = lambda i: (i // tpb if bb == 1 else i, i % tpb if bb == 1 else 0, 0)
    mmap = lambda i: (i // tpb if bb == 1 else i, 0, 0)
    hw = d // 2
    wspec = lambda shp: pl.BlockSpec(shp, lambda i: (0, 0), pipeline_mode=pl.Buffered(1))
    return pl.pallas_call(
        _mix_kernel,
        grid=(ntok // tm,),
        in_specs=[pl.BlockSpec((tm, d), lambda i: (i, 0)),
                  pl.BlockSpec((tm, d), lambda i: (i, 0)),
                  pl.BlockSpec((tm, hw), lambda i: (i, COL_GA // hw)),
                  pl.BlockSpec((tm, hw), lambda i: (i, COL_GA // hw + 1)),
                  pl.BlockSpec((tm, hw), lambda i: (i, COL_GB // hw)),
                  pl.BlockSpec((tm, hw), lambda i: (i, COL_GB // hw + 1)),
                  pl.BlockSpec((bb, tt, d), xmap),
                  pl.BlockSpec((bb, 1, d), mmap),
                  pl.BlockSpec((bb, 1, d), mmap),
                  pl.BlockSpec((bb, 1, d), mmap),
                  wspec(wa.shape), wspec(wb.shape), wspec(wo.shape),
                  pl.BlockSpec((1, d), lambda i: (0, 0)),
                  pl.BlockSpec((1, d), lambda i: (0, 0))],
        out_specs=[pl.BlockSpec((bb, tt, d), xmap),
                   pl.BlockSpec((d, tm), lambda i: (0, i))],
        out_shape=[jax.ShapeDtypeStruct((bx, tx, d), F32),
                   jax.ShapeDtypeStruct((d, ntok), BF16)],
        compiler_params=_cparams(("parallel",)),
        name="mix",
    )(att2, hm2, p2, p2, p2, p2, x, g1, sc2, sh2, wa, wb, wo, ln_g, ln_b)


def _top16(val, kio):
    work = val
    rank = jnp.full(val.shape, float(TOPK), F32)
    tops = []
    nkeys = val.shape[0]
    for r in range(TOPK):
        m = jnp.max(work, axis=0, keepdims=True)
        idx = jnp.min(jnp.where(work == m, kio, float(nkeys)), axis=0, keepdims=True)
        hit = kio == idx
        rank = jnp.where(hit, float(r), rank)
        work = jnp.where(hit, -jnp.inf, work)
        tops.append(m)
    return rank, tops


def _top16_distinct(val, want_rank):
    work = val
    rank = jnp.full(val.shape, float(TOPK), F32) if want_rank else None
    tops = []
    for r in range(TOPK):
        m = jnp.max(work, axis=0, keepdims=True)
        hit = work == m
        if want_rank:
            rank = jnp.where(hit, float(r), rank)
        work = jnp.where(hit, -jnp.inf, work)
        tops.append(m)
    removed = jnp.sum(jnp.where(work == -jnp.inf, 1.0, 0.0), axis=0, keepdims=True)
    return rank, tops, removed


def _peer_select_kernel(h2t_ref, wq_ref, keys_ref, rb_ref, eb_ref, cnt_ref, ea_ref):
    tl = h2t_ref.shape[1]
    w = min(tl, 128)
    kio = lax.broadcasted_iota(jnp.int32, (NK, w), 0).astype(F32)
    rio = lax.broadcasted_iota(jnp.int32, (TOPK, w), 0).astype(F32)

    def scores(hd):
        qp = jnp.dot(wq_ref[hd * 2 * NK:(hd + 1) * 2 * NK, :], h2t_ref[...],
                     preferred_element_type=F32).astype(BF16)
        return (jnp.dot(keys_ref[2 * hd], qp[:NK], preferred_element_type=F32),
                jnp.dot(keys_ref[2 * hd + 1], qp[NK:], preferred_element_type=F32))

    def head_lanes(hd, ls, a, b, exact):
        if exact:
            ra, atop = _top16(a, kio)
            rb, btop = _top16(b, kio)
            tied = None
        else:
            _, atop, na = _top16_distinct(a, False)
            rb, btop, nb = _top16_distinct(b, True)
            tied = jnp.max(jnp.maximum(na, nb)) > float(TOPK)
        asort = jnp.concatenate(atop, axis=0)
        cnt = jnp.zeros((TOPK, w), F32)
        front = asort + btop[0]
        for _ in range(TOPK):
            m = jnp.max(front, axis=0, keepdims=True)
            idx = jnp.min(jnp.where(front == m, rio, float(TOPK)), axis=0, keepdims=True)
            hit = rio == idx
            cnt = cnt + jnp.where(hit, 1.0, 0.0)
            hr = TOPK // 2
            nxt = jnp.full((hr, w), -jnp.inf, F32)
            for cc in range(1, TOPK):
                nxt = jnp.where(cnt[:hr] == float(cc), btop[cc], nxt)
            nxt = jnp.concatenate([asort[:hr] + nxt, jnp.full((TOPK - hr, w), -jnp.inf, F32)], axis=0)
            front = jnp.where(hit, nxt, front)
        ea_s = jnp.exp(asort - atop[0])
        pref = jnp.zeros((1, w), F32)
        pbsel = jnp.zeros((TOPK, w), F32)
        for cc in range(1, TOPK + 1):
            pref = pref + jnp.exp(btop[cc - 1] - btop[0])
            pbsel = jnp.where(cnt == float(cc), pref, pbsel)
        z = jnp.sum(ea_s * pbsel, axis=0, keepdims=True)
        cnti = jnp.zeros((NK, w), F32)
        for r in range(TOPK):
            match = (ra == float(r)) if exact else (a == atop[r])
            cnti = jnp.where(match, cnt[r:r + 1, :], cnti)
        rb_ref[hd, :, ls] = rb.astype(rb_ref.dtype)
        eb_ref[hd, :, ls] = jnp.exp(b - btop[0]).astype(eb_ref.dtype)
        cnt_ref[hd, :, ls] = cnti
        ea_ref[hd, :, ls] = jnp.exp(a - atop[0]) * (0.5 / z)
        return tied

    def head(hd, a, b, exact):
        tied = None
        for c0 in range(0, tl, w):
            ls = slice(c0, c0 + w)
            t = head_lanes(hd, ls, a[:, ls], b[:, ls], exact)
            if t is not None:
                tied = t if tied is None else jnp.logical_or(tied, t)
        return tied

    tied = []
    nxt = scores(0)
    for hd in range(PEER_H):
        a, b = nxt
        if hd + 1 < PEER_H:
            nxt = scores(hd + 1)
        tied.append(head(hd, a, b, False))

    for hd in range(PEER_H):
        @pl.when(tied[hd])
        def _(hd=hd):
            head(hd, *scores(hd), True)


def _peer_select(h2t, wqt, keys, tl):
    d, ntok = h2t.shape
    out = jax.ShapeDtypeStruct((PEER_H, NK, ntok), F32)
    outb = jax.ShapeDtypeStruct((PEER_H, NK, ntok), jnp.bfloat16)
    ospec = pl.BlockSpec((PEER_H, NK, tl), lambda i: (0, 0, i))
    return pl.pallas_call(
        _peer_select_kernel,
        grid=(ntok // tl,),
        in_specs=[pl.BlockSpec((d, tl), lambda i: (0, i)),
                  pl.BlockSpec(wqt.shape, lambda i: (0, 0), pipeline_mode=pl.Buffered(1)),
                  pl.BlockSpec(keys.shape, lambda i: (0, 0, 0))],
        out_specs=[ospec, ospec, ospec, ospec],
        out_shape=[outb, outb, out, out],
        compiler_params=_cparams(("parallel",)),
        name="peer_select",
    )(h2t, wqt, keys)


def _peer_dense_kernel(h2t_ref, u_ref, vt_ref, rb_ref, eb_ref, cnt_ref, ea_ref,
                       x1_ref, g2_ref, lg_ref, lb_ref, y_ref,
                       o_ref, act0_ref, act1_ref, mt_ref, *, n_e):
    s = pl.program_id(0)
    e_prev = jnp.maximum(s - 1, 0) % n_e
    half = u_ref.shape[0] // 2
    zero = jnp.zeros((), rb_ref.dtype)

    @pl.when(s == 0)
    def _():
        act1_ref[...] = jnp.zeros_like(act1_ref)

    @pl.when(e_prev == 0)
    def _():
        o_ref[...] = jnp.zeros_like(o_ref)

    tm = h2t_ref.shape[1]
    pk = rb_ref.shape[2]

    def row(ref, hd, il):
        return jnp.broadcast_to(ref[hd, il:il + 1, :], (pk, tm)).astype(rb_ref.dtype)[None]

    def gated(old_ref, hf):
        for il in range(hf * half // NK, (hf + 1) * half // NK):
            gate = None
            for hd in range(PEER_H):
                term = jnp.where(rb_ref[hd] < row(cnt_ref, hd, il), eb_ref[hd] * row(ea_ref, hd, il), zero)
                gate = term if gate is None else gate + term
            rows = slice(il * NK, (il + 1) * NK)
            a = old_ref[rows, :]
            half_gelu2 = a * (1.0 + lax.erf(a * (2.0 ** -0.5)))
            mt_ref[rows, :] = (gate.reshape(NK, tm) * half_gelu2.astype(gate.dtype)).astype(mt_ref.dtype)

    def phases(new_ref, old_ref):
        new_ref[...] = jnp.dot(u_ref[...], h2t_ref[...], preferred_element_type=F32)
        for hf in range(2):
            rows = slice(hf * half, (hf + 1) * half)
            gated(old_ref, hf)
            o_ref[...] += jnp.dot(vt_ref[:, rows], mt_ref[rows, :], preferred_element_type=F32)

    @pl.when(s % 2 == 0)
    def _():
        phases(act0_ref, act1_ref)

    @pl.when(s % 2 == 1)
    def _():
        phases(act1_ref, act0_ref)

    @pl.when((s > 0) & (e_prev == n_e - 1))
    def _():
        shp = x1_ref.shape
        r = ALPHA * x1_ref[...] + g2_ref[...] * o_ref[...].T.reshape(shp)
        y_ref[...] = _layer_norm(r, lg_ref[...], lb_ref[...])


def _peer_dense(h2t, u, vt, rb, eb, cnti, ea, x1, g2, ln_g, ln_b, bb, tt):
    d, ntok = h2t.shape
    bx, tx, _ = x1.shape
    tm = bb * tt
    tpb = tx // tt
    et = 8 * NK
    n_e = u.shape[0] // et
    n_items = (ntok // tm) * n_e
    item_a = lambda s: jnp.minimum(s, n_items - 1)
    item_b = lambda s: jnp.maximum(s - 1, 0)
    pk = 16
    rb = rb.reshape(PEER_H, NK // pk, pk, ntok)
    eb = eb.reshape(PEER_H, NK // pk, pk, ntok)
    rspec = pl.BlockSpec((PEER_H, NK // pk, pk, tm), lambda s: (0, 0, 0, item_b(s) // n_e))
    cspec = pl.BlockSpec((PEER_H, 8, tm), lambda s: (0, item_b(s) % n_e, item_b(s) // n_e))
    tile_b = lambda s: item_b(s) // n_e
    xmap = lambda s: (tile_b(s) // tpb if bb == 1 else tile_b(s), tile_b(s) % tpb if bb == 1 else 0, 0)
    mmap = lambda s: (tile_b(s) // tpb if bb == 1 else tile_b(s), 0, 0)
    return pl.pallas_call(
        functools.partial(_peer_dense_kernel, n_e=n_e),
        grid=(n_items + 1,),
        in_specs=[pl.BlockSpec((d, tm), lambda s: (0, item_a(s) // n_e)),
                  pl.BlockSpec((et, d), lambda s: (item_a(s) % n_e, 0)),
                  pl.BlockSpec((d, et), lambda s: (0, item_b(s) % n_e)),
                  rspec, rspec, cspec, cspec,
                  pl.BlockSpec((bb, tt, d), xmap),
                  pl.BlockSpec((bb, 1, d), mmap),
                  pl.BlockSpec((1, d), lambda s: (0, 0)),
                  pl.BlockSpec((1, d), lambda s: (0, 0))],
        out_specs=pl.BlockSpec((bb, tt, d), xmap),
        out_shape=jax.ShapeDtypeStruct((bx, tx, d), F32),
        scratch_shapes=[pltpu.VMEM((d, tm), F32), pltpu.VMEM((et, tm), F32),
                        pltpu.VMEM((et, tm), F32), pltpu.VMEM((et, tm), BF16)],
        compiler_params=_cparams(("arbitrary",)),
        name="peer_dense",
    )(h2t, u, vt, rb, eb, cnti, ea, x1, g2, ln_g, ln_b)


def _tile(bx, tx, want):
    if tx >= want:
        return 1, want
    return want // tx, tx


def _layer(x, ada, state, prm):
    bx, tx, d = x.shape
    ntok = bx * tx
    sh1, sc1, g1, sh2, sc2, g2 = [a[:, None, :] for a in jnp.split(ada, 6, axis=-1)]

    bb, tt = _tile(bx, tx, min(1024, ntok))
    p2, gates2 = _inproj(x, sc1, sh1, prm["w_in1"], prm["w_in2"], prm["w_if"], bb, tt)
    p3 = p2.reshape(bx, tx, IN_MAIN)

    if state is None:
        att = _attn_prompt(p3, prm["bias_prompt"], prm["sink_prompt"])
        keep = min(WINDOW, tx)
        k_keep = p3[:, tx - keep:, COL_AK:COL_AK + ATT_KV_W].astype(F32)
        v_keep = p3[:, tx - keep:, COL_AV:COL_AV + ATT_KV_W].astype(F32)
        conv_buf = jnp.zeros((bx, ML_CONV - 1, 2 * ML_QK_W), F32)
        c0 = jnp.zeros((bx, ML_H, ML_DK, ML_DV), F32)
        n0 = jnp.zeros((bx, ML_H, ML_DK), F32)
        m0 = jnp.zeros((bx, ML_H), F32)
    else:
        kwin, vwin, conv_buf, c0, n0, m0 = state
        wc = kwin.shape[1]
        att, k_keep, v_keep = _attn_sample(
            p3, kwin.reshape(bx, wc, ATT_KV_W), vwin.reshape(bx, wc, ATT_KV_W),
            prm["bias_sample_w"], prm["bias_sample_n"], prm["sink_sample"], math.gcd(bx, 8))
    k_keep = k_keep.reshape(bx, -1, ATT_KV, HD)
    v_keep = v_keep.reshape(bx, -1, ATT_KV, HD)

    L = min(ML_CHUNK_MAX, tx)
    nc = tx // L
    gates = gates2[:, :2 * ML_H].reshape(bx * nc, L, 2, ML_H)
    gates_col = jnp.transpose(gates, (0, 3, 1, 2))
    gates_row = jnp.transpose(gates, (0, 3, 2, 1))
    conv8 = jnp.pad(conv_buf.astype(F32), ((0, 0), (8 - (ML_CONV - 1), 0), (0, 0)))
    hm, c1, n1, m1 = _mlstm(p3, gates_col, gates_row, conv8, prm["conv_w"], prm["conv_b"],
                            prm["norm_g"], prm["b_i"], prm["b_f"], c0.astype(F32),
                            n0.astype(F32).reshape(bx, ML_H, 1, ML_DK),
                            m0.astype(F32).reshape(bx, ML_H, 1, 1), L,
                            math.gcd(bx, 8) if nc == 1 else 1)
    conv_keep = p3[:, tx - (ML_CONV - 1):, COL_MQK:COL_MQK + 2 * ML_QK_W].astype(F32)
    new_state = (k_keep, v_keep, conv_keep, c1, n1.reshape(bx, ML_H, ML_DK), m1.reshape(bx, ML_H))

    bb, tt = _tile(bx, tx, min(256, ntok))
    x1, h2t = _mix(att.reshape(ntok, ATT_Q_W), hm.reshape(ntok, ML_V_W), p2, x, g1, sc2, sh2,
                   prm["w_a"], prm["w_b"], prm["w_out"], prm["ln1_g"], prm["ln1_b"], bb, tt)

    rb, eb, cnti, ea = _peer_select(h2t, prm["w_pqt"], prm["keys"], min(256, ntok))
    bb, tt = _tile(bx, tx, min(512, ntok))
    y = _peer_dense(h2t, prm["peer_u"], prm["peer_vt"], rb, eb, cnti, ea,
                    x1, g2, prm["ln2_g"], prm["ln2_b"], bb, tt)
    return y, new_state


def kernel(x_prompt, x_sample, c_prompt, c_sample, cache_k_win, cache_v_win, state_conv, state_C, state_n, state_m, rel_bias_table, w_ada, b_ada, w_in, b_i, b_f, conv_w, conv_b, ml_norm_g, att_sinks, w_a, w_b, w_out, ln1_g, ln1_b, w_pq, peer_sub_keys, peer_u, peer_v, ln2_g, ln2_b):
    assert w_ada.shape[0] == DEPTH
    d = x_prompt.shape[-1]
    bp = x_prompt.shape[0]
    ts = x_sample.shape[1]
    wc = cache_k_win.shape[2]

    wi = w_in[0]
    n_if = 2 * ML_H
    assert wi.shape[1] == IN_MAIN + n_if
    w_in1 = wi.astype(BF16)
    w_in2 = w_in1[:, IN_W1 + n_if:]
    w_if = jnp.pad(wi[:, IN_W1:IN_W1 + n_if], ((0, 0), (0, IF_PAD - n_if))).astype(BF16)

    sinks = att_sinks[0].astype(F32).reshape(ATT_KV, ATT_G, 1, 1)
    blk = WINDOW
    dist_p = (jnp.arange(blk)[:, None] + blk) - jnp.arange(2 * blk)[None, :]
    dist_s = (jnp.arange(ts)[:, None] + wc) - jnp.arange(wc + ts)[None, :]
    bias_s = _bias_table(rel_bias_table, dist_s)
    bias_p = _bias_table_t(rel_bias_table, dist_p)
    prm = {
        "w_in1": w_in1, "w_in2": w_in2, "w_if": w_if,
        "bias_prompt": jnp.stack([jnp.where(jnp.arange(2 * blk)[None, :, None] < blk, NEG, bias_p), bias_p]),
        "sink_prompt": jnp.broadcast_to(att_sinks[0].astype(F32)[:, None, None], (ATT_HEADS, 1, blk)),
        "bias_sample_w": bias_s[:, :, :wc].reshape(ATT_HEADS * ts, wc),
        "bias_sample_n": bias_s[:, :, wc:].reshape(ATT_HEADS * ts, ts),
        "sink_sample": jnp.broadcast_to(sinks, (ATT_KV, ATT_G, ts, 1)).reshape(ATT_HEADS * ts, 1),
        "conv_w": conv_w[0], "conv_b": conv_b[0].reshape(1, -1), "norm_g": ml_norm_g[0].reshape(1, -1),
        "b_i": b_i[0], "b_f": b_f[0],
        "w_a": w_a[0].astype(BF16), "w_b": w_b[0].astype(BF16), "w_out": w_out[0].astype(BF16),
        "ln1_g": ln1_g[0].reshape(1, -1), "ln1_b": ln1_b[0].reshape(1, -1),
        "w_pqt": w_pq[0].T.astype(BF16),
        "keys": peer_sub_keys[0].reshape(2 * PEER_H, NK, -1).astype(BF16),
        "peer_u": peer_u[0].astype(BF16),
        "peer_vt": peer_v[0].T.astype(BF16),
        "ln2_g": ln2_g[0].reshape(1, -1), "ln2_b": ln2_b[0].reshape(1, -1),
    }

    ada = _ada(jnp.concatenate([c_prompt, c_sample], axis=0), w_ada[0], b_ada[0])
    yp, sp = _layer(x_prompt, ada[:bp], None, prm)
    ys, ss = _layer(x_sample, ada[bp:],
                    (cache_k_win[0], cache_v_win[0], state_conv[0], state_C[0], state_n[0], state_m[0]), prm)
    stack = lambda s: [a[None] for a in s]
    return (yp, ys, *stack(sp), *stack(ss))
```
